```python
import math
import jax, jax.numpy as jnp
from jax import lax
import numpy as np

D_MODEL = 1024
BATCH = 8
SEQ = 2048
DEPTH = 1
DEC_BATCH = 128
DEC_SEQ = 8
PAST_LEN = 16384
PAGE_SIZE = 128

N_MEM = 256
D_A = 384
D_B = 384
N_XHEADS = 4
XHEAD_DIM = 64
D_X = N_XHEADS * XHEAD_DIM
D_MIX = D_A + D_B + D_X
CONV_A = 3
CONV_B = 31
SPLIT_SIZES = (D_A, D_A, D_A, D_A, D_B, D_B, D_B, D_X, D_X)
D_IN = 4 * D_A + 3 * D_B + 2 * D_X
EPS = 1e-6

kernel_name = "hybrid_shortconv_conformer_memxattn_step"


def rmsnorm(x, g):
    xf = x.astype(jnp.float32)
    out = xf * lax.rsqrt(jnp.mean(xf * xf, axis=-1, keepdims=True) + EPS)
    return (out * g.astype(jnp.float32)).astype(x.dtype)


def layernorm(x, g, b):
    xf = x.astype(jnp.float32)
    mu = jnp.mean(xf, axis=-1, keepdims=True)
    xc = xf - mu
    out = xc * lax.rsqrt(jnp.mean(xc * xc, axis=-1, keepdims=True) + EPS)
    return (out * g.astype(jnp.float32) + b.astype(jnp.float32)).astype(x.dtype)


def causal_dwconv(ctx, x, w, b):
    k = w.shape[0]
    xp = jnp.concatenate([ctx, x], axis=1)
    y = lax.conv_general_dilated(
        xp, w[:, None, :].astype(xp.dtype), window_strides=(1,), padding='VALID',
        dimension_numbers=('NWC', 'WIO', 'NWC'), feature_group_count=x.shape[-1])
    return y + b, xp[:, xp.shape[1] - (k - 1):]


def mem_kv(mem, g_mem, w_mk, w_mv):
    m = rmsnorm(mem, g_mem)
    bsz = mem.shape[0]
    k = (m @ w_mk).reshape(bsz, N_MEM, N_XHEADS, XHEAD_DIM)
    v = (m @ w_mv).reshape(bsz, N_MEM, N_XHEADS, XHEAD_DIM)
    return k, v


def cross_attn(q, k, v):
    scores = jnp.einsum('bshd,bmhd->bhsm', q, k).astype(jnp.float32) * (XHEAD_DIM ** -0.5)
    p = jax.nn.softmax(scores, axis=-1).astype(v.dtype)
    out = jnp.einsum('bhsm,bmhd->bshd', p, v)
    return out.reshape(q.shape[0], q.shape[1], D_X)


def mixer_layer(x, ctx_a, ctx_b, mk, mv, g_norm, w_in, w_conv_a, b_conv_a,
                w_conv_b, b_conv_b, ln_g, ln_b, w_out):
    bsz, slen, _ = x.shape
    h = rmsnorm(x, g_norm)
    proj = h @ w_in
    parts = []
    off = 0
    for sz in SPLIT_SIZES:
        parts.append(proj[..., off:off + sz])
        off += sz
    bg_a, cg_a, v_a, z_a, a_b, g_b, z_b, q_x, z_x = parts
    conv_a, new_ctx_a = causal_dwconv(ctx_a, cg_a * v_a, w_conv_a, b_conv_a)
    out_a = bg_a * conv_a * jax.nn.silu(z_a)
    u_b = a_b * jax.nn.sigmoid(g_b)
    conv_b, new_ctx_b = causal_dwconv(ctx_b, u_b, w_conv_b, b_conv_b)
    out_b = jax.nn.silu(layernorm(conv_b, ln_g, ln_b)) * jax.nn.silu(z_b)
    q = q_x.reshape(bsz, slen, N_XHEADS, XHEAD_DIM)
    out_x = cross_attn(q, mk, mv) * jax.nn.silu(z_x)
    mixed = jnp.concatenate([out_a, out_b, out_x], axis=-1)
    return x + mixed @ w_out, new_ctx_a, new_ctx_b


def setup_inputs(seed: int = 0) -> dict:
    key = jax.random.key(seed)
    ks = jax.random.split(key, 24)
    f32 = jnp.float32
    nrm = lambda k, shape, s: jax.random.normal(k, shape, f32) * s
    return {
        "x_prompt": nrm(ks[0], (BATCH, SEQ, D_MODEL), 1.0),
        "x_sample": nrm(ks[1], (DEC_BATCH, DEC_SEQ, D_MODEL), 1.0),
        "state_conv_a": nrm(ks[2], (DEPTH, DEC_BATCH, CONV_A - 1, D_A), 1.0),
        "state_conv_b": nrm(ks[3], (DEPTH, DEC_BATCH, CONV_B - 1, D_B), 0.5),
        "cache_mem_k": nrm(ks[4], (DEPTH, DEC_BATCH, N_MEM, N_XHEADS, XHEAD_DIM), 1.0),
        "cache_mem_v": nrm(ks[5], (DEPTH, DEC_BATCH, N_MEM, N_XHEADS, XHEAD_DIM), 1.0),
        "mem_prompt": nrm(ks[6], (BATCH, N_MEM, D_MODEL), 1.0),
        "g_norm": 1.0 + nrm(ks[7], (DEPTH, D_MODEL), 0.02),
        "w_in": nrm(ks[8], (DEPTH, D_MODEL, D_IN), D_MODEL ** -0.5),
        "w_conv_a": nrm(ks[9], (DEPTH, CONV_A, D_A), CONV_A ** -0.5),
        "b_conv_a": nrm(ks[10], (DEPTH, D_A), 0.01),
        "w_conv_b": nrm(ks[11], (DEPTH, CONV_B, D_B), CONV_B ** -0.5),
        "b_conv_b": nrm(ks[12], (DEPTH, D_B), 0.01),
        "ln_g": 1.0 + nrm(ks[13], (DEPTH, D_B), 0.02),
        "ln_b": nrm(ks[14], (DEPTH, D_B), 0.01),
        "w_out": nrm(ks[15], (DEPTH, D_MIX, D_MODEL), D_MIX ** -0.5),
        "g_mem": 1.0 + nrm(ks[16], (DEPTH, D_MODEL), 0.02),
        "w_mem_k": nrm(ks[17], (DEPTH, D_MODEL, D_X), D_MODEL ** -0.5),
        "w_mem_v": nrm(ks[18], (DEPTH, D_MODEL, D_X), D_MODEL ** -0.5),
        "g_final": 1.0 + nrm(ks[19], (D_MODEL,), 0.02),
    }


def reference(x_prompt, x_sample, state_conv_a, state_conv_b, cache_mem_k, cache_mem_v,
              mem_prompt, g_norm, w_in, w_conv_a, b_conv_a, w_conv_b, b_conv_b,
              ln_g, ln_b, w_out, g_mem, w_mem_k, w_mem_v, g_final):
    yp, ys = x_prompt, x_sample
    bsz = x_prompt.shape[0]
    pa_list, pb_list, pk_list, pv_list, sa_list, sb_list = [], [], [], [], [], []
    for l in range(DEPTH):
        params = (g_norm[l], w_in[l], w_conv_a[l], b_conv_a[l], w_conv_b[l], b_conv_b[l],
                  ln_g[l], ln_b[l], w_out[l])
        mk_p, mv_p = mem_kv(mem_prompt, g_mem[l], w_mem_k[l], w_mem_v[l])
        ctx0_a = jnp.zeros((bsz, CONV_A - 1, D_A), yp.dtype)
        ctx0_b = jnp.zeros((bsz, CONV_B - 1, D_B), yp.dtype)
        yp, pa, pb = mixer_layer(yp, ctx0_a, ctx0_b, mk_p, mv_p, *params)
        ys, sa, sb = mixer_layer(ys, state_conv_a[l], state_conv_b[l],
                                 cache_mem_k[l], cache_mem_v[l], *params)
        pa_list.append(pa); pb_list.append(pb); pk_list.append(mk_p); pv_list.append(mv_p)
        sa_list.append(sa); sb_list.append(sb)
    y_prompt = rmsnorm(yp, g_final)
    y_sample = rmsnorm(ys, g_final)
    return (y_prompt, y_sample,
            jnp.stack(pa_list), jnp.stack(pb_list), jnp.stack(pk_list), jnp.stack(pv_list),
            jnp.stack(sa_list), jnp.stack(sb_list))
```

```python
import functools

import jax
import jax.numpy as jnp
from jax import lax
from jax.experimental import pallas as pl
from jax.experimental.pallas import tpu as pltpu

D_MODEL = 1024
D_A = 384
D_B = 384
N_XHEADS = 4
XHEAD_DIM = 64
D_X = N_XHEADS * XHEAD_DIM
N_MEM = 256
CONV_A = 3
CONV_B = 31
EPS = 1e-6
ATTN_SCALE = XHEAD_DIM ** -0.5

SUBLANES = 8

CTX_A_ROWS = SUBLANES
CTX_B_ROWS = 4 * SUBLANES

PROMPT_TILE = 256
SAMPLE_GROUP = 16
VMEM_LIMIT_BYTES = 56 * 1024 * 1024

F32 = jnp.float32
BF16 = jnp.bfloat16


def _rmsnorm(x, g):
    ms = jnp.mean(x * x, axis=-1, keepdims=True)
    return x * lax.rsqrt(ms + EPS) * g


def _layernorm(x, g, b):
    mu = jnp.mean(x, axis=-1, keepdims=True)
    xc = x - mu
    var = jnp.mean(xc * xc, axis=-1, keepdims=True)
    return xc * lax.rsqrt(var + EPS) * g + b


def _silu(x):
    return x * jax.nn.sigmoid(x)


def _head_of_lane(shape, axis):
    return lax.broadcasted_iota(jnp.int32, shape, axis) // XHEAD_DIM


def _softmax_unnormalised(s):
    m = jnp.max(s, axis=-1, keepdims=True)
    e = jnp.exp(s - m)
    return e, jnp.sum(e, axis=-1, keepdims=True)


def _mix_tail(x, out_a, out_b, out_x, wout_ref, gfin_ref):
    mixed = jnp.concatenate(
        [out_a.astype(BF16), out_b.astype(BF16), out_x.astype(BF16)], axis=-1)
    y = x + jnp.dot(mixed, wout_ref[...], preferred_element_type=F32)
    return _rmsnorm(y, gfin_ref[...])


def _prompt_kernel(x_ref, mem_ref, gnorm_ref, wa_ref, wb_ref, wx_ref, wca_ref, bca_ref,
                   wcb_ref, bcb_ref, lng_ref, lnb_ref, wout_ref, gmem_ref, wmkv_ref,
                   gfin_ref,
                   y_ref, sa_ref, sb_ref, ko_ref, vo_ref,
                   xa_ref, xb_ref, kblk_ref, vblk_ref):
    j = pl.program_id(1)
    n_tiles = pl.num_programs(1)
    t = x_ref.shape[0]

    @pl.when(j == 0)
    def _start_of_sequence():
        xa_ref[pl.ds(0, CTX_A_ROWS), :] = jnp.zeros((CTX_A_ROWS, D_A), F32)
        xb_ref[pl.ds(0, CTX_B_ROWS), :] = jnp.zeros((CTX_B_ROWS, D_B), F32)
        m = _rmsnorm(mem_ref[...], gmem_ref[...]).astype(BF16)
        kv = jnp.dot(m, wmkv_ref[...], preferred_element_type=F32)
        k = kv[:, :D_X]
        v = kv[:, D_X:]
        ko_ref[...] = k
        vo_ref[...] = v
        kt = k.T
        row_head = _head_of_lane((D_X, N_MEM), 0)
        col_head = _head_of_lane((N_MEM, D_X), 1)
        for h in range(N_XHEADS):
            kblk_ref[:, h * N_MEM:(h + 1) * N_MEM] = jnp.where(
                row_head == h, kt, 0.0).astype(BF16)
            vblk_ref[h * N_MEM:(h + 1) * N_MEM, :] = jnp.where(
                col_head == h, v, 0.0).astype(BF16)

    x = x_ref[...]
    h_in = _rmsnorm(x, gnorm_ref[...]).astype(BF16)

    pa = jnp.dot(h_in, wa_ref[...], preferred_element_type=F32)
    bg = pa[:, 0 * D_A:1 * D_A]
    cg = pa[:, 1 * D_A:2 * D_A]
    va = pa[:, 2 * D_A:3 * D_A]
    za = pa[:, 3 * D_A:4 * D_A]
    xa_ref[pl.ds(CTX_A_ROWS, t), :] = cg * va
    conv_a = bca_ref[...]
    for kk in range(CONV_A):
        conv_a = conv_a + wca_ref[kk:kk + 1, :] * xa_ref[
            pl.ds(CTX_A_ROWS - (CONV_A - 1) + kk, t), :]
    out_a = bg * conv_a * _silu(za)

    pb = jnp.dot(h_in, wb_ref[...], preferred_element_type=F32)
    ab = pb[:, 0 * D_B:1 * D_B]
    gb = pb[:, 1 * D_B:2 * D_B]
    zb = pb[:, 2 * D_B:3 * D_B]
    xb_ref[pl.ds(CTX_B_ROWS, t), :] = ab * jax.nn.sigmoid(gb)
    conv_b = bcb_ref[...]
    for kk in range(CONV_B):
        conv_b = conv_b + wcb_ref[kk:kk + 1, :] * xb_ref[
            pl.ds(CTX_B_ROWS - (CONV_B - 1) + kk, t), :]
    out_b = _silu(_layernorm(conv_b, lng_ref[...], lnb_ref[...])) * _silu(zb)

    px = jnp.dot(h_in, wx_ref[...], preferred_element_type=F32)
    q = (px[:, :D_X] * ATTN_SCALE).astype(BF16)
    zx = px[:, D_X:]
    s_all = jnp.dot(q, kblk_ref[...], preferred_element_type=F32)
    lane_head = _head_of_lane((t, D_X), 1)
    probs = []
    inv_l = jnp.zeros((t, D_X), F32)
    for h in range(N_XHEADS):
        e, l = _softmax_unnormalised(s_all[:, h * N_MEM:(h + 1) * N_MEM])
        probs.append(e.astype(BF16))
        inv_l = jnp.where(lane_head == h, 1.0 / l, inv_l)
    p_all = jnp.concatenate(probs, axis=-1)
    out_x = jnp.dot(p_all, vblk_ref[...], preferred_element_type=F32) * inv_l * _silu(zx)

    y_ref[...] = _mix_tail(x, out_a, out_b, out_x, wout_ref, gfin_ref)

    @pl.when(j == n_tiles - 1)
    def _emit_state():
        sa_ref[...] = xa_ref[pl.ds(t + CTX_A_ROWS - (CONV_A - 1), CONV_A - 1), :]
        sb_ref[...] = xb_ref[pl.ds(t + CTX_B_ROWS - (CONV_B - 1), CONV_B - 1), :]

    @pl.when(j < n_tiles - 1)
    def _carry_context():
        xa_ref[pl.ds(0, CTX_A_ROWS), :] = xa_ref[pl.ds(t, CTX_A_ROWS), :]
        xb_ref[pl.ds(0, CTX_B_ROWS), :] = xb_ref[pl.ds(t, CTX_B_ROWS), :]


def _prompt_call(x, mem, gnorm, wa, wb, wx, wca, bca, wcb, bcb, lng, lnb, wout, gmem, wmkv,
                 gfin):
    bsz, seq, _ = x.shape
    t = PROMPT_TILE
    n_tiles = seq // t
    const = lambda shape: pl.BlockSpec(shape, lambda b, j: (0,) * len(shape))
    in_specs = [
        pl.BlockSpec((None, t, D_MODEL), lambda b, j: (b, j, 0)),
        pl.BlockSpec((None, N_MEM, D_MODEL), lambda b, j: (b, 0, 0)),
        const((1, D_MODEL)),
        const((D_MODEL, 4 * D_A)),
        const((D_MODEL, 3 * D_B)),
        const((D_MODEL, 2 * D_X)),
        const((CONV_A, D_A)),
        const((1, D_A)),
        const((CONV_B, D_B)),
        const((1, D_B)),
        const((1, D_B)),
        const((1, D_B)),
        const((D_MODEL, D_MODEL)),
        const((1, D_MODEL)),
        const((D_MODEL, 2 * D_X)),
        const((1, D_MODEL)),
    ]
    out_shape = (
        jax.ShapeDtypeStruct((bsz, seq, D_MODEL), F32),
        jax.ShapeDtypeStruct((bsz, CONV_A - 1, D_A), F32),
        jax.ShapeDtypeStruct((bsz, CONV_B - 1, D_B), F32),
        jax.ShapeDtypeStruct((bsz, N_MEM, D_X), F32),
        jax.ShapeDtypeStruct((bsz, N_MEM, D_X), F32),
    )
    out_specs = (
        pl.BlockSpec((None, t, D_MODEL), lambda b, j: (b, j, 0)),
        pl.BlockSpec((None, CONV_A - 1, D_A), lambda b, j: (b, 0, 0)),
        pl.BlockSpec((None, CONV_B - 1, D_B), lambda b, j: (b, 0, 0)),
        pl.BlockSpec((None, N_MEM, D_X), lambda b, j: (b, 0, 0)),
        pl.BlockSpec((None, N_MEM, D_X), lambda b, j: (b, 0, 0)),
    )
    scratch_shapes = [
        pltpu.VMEM((t + CTX_A_ROWS, D_A), F32),
        pltpu.VMEM((t + CTX_B_ROWS, D_B), F32),
        pltpu.VMEM((D_X, N_XHEADS * N_MEM), BF16),
        pltpu.VMEM((N_XHEADS * N_MEM, D_X), BF16),
    ]
    return pl.pallas_call(
        _prompt_kernel,
        out_shape=out_shape,
        grid=(bsz, n_tiles),
        in_specs=in_specs,
        out_specs=out_specs,
        scratch_shapes=scratch_shapes,
        compiler_params=pltpu.CompilerParams(
            dimension_semantics=("arbitrary", "arbitrary"),
            vmem_limit_bytes=VMEM_LIMIT_BYTES),
        name="prompt_mixer",
    )(x, mem, gnorm, wa, wb, wx, wca, bca, wcb, bcb, lng, lnb, wout, gmem, wmkv, gfin)


def _sample_kernel(x_ref, sa_in_ref, sb_in_ref, kc_ref, vc_ref, gnorm_ref, wa_ref, wb_ref,
                   wx_ref, wca_ref, bca_ref, wcb_ref, bcb_ref, lng_ref, lnb_ref, wout_ref,
                   gfin_ref,
                   y_ref, sa_ref, sb_ref,
                   xa_ref, xb_ref, ox_ref):
    g, s, _ = x_ref.shape
    t = g * s
    x = x_ref[...].reshape(t, D_MODEL)
    h_in = _rmsnorm(x, gnorm_ref[...]).astype(BF16)

    pa = jnp.dot(h_in, wa_ref[...], preferred_element_type=F32)
    bg = pa[:, 0 * D_A:1 * D_A]
    cg = pa[:, 1 * D_A:2 * D_A]
    va = pa[:, 2 * D_A:3 * D_A]
    za = pa[:, 3 * D_A:4 * D_A]
    xa_ref[:, pl.ds(CTX_A_ROWS - (CONV_A - 1), CONV_A - 1), :] = sa_in_ref[...]
    xa_ref[:, pl.ds(CTX_A_ROWS, s), :] = (cg * va).reshape(g, s, D_A)
    conv_a = bca_ref[...]
    for kk in range(CONV_A):
        win = xa_ref[:, pl.ds(CTX_A_ROWS - (CONV_A - 1) + kk, s), :].reshape(t, D_A)
        conv_a = conv_a + wca_ref[kk:kk + 1, :] * win
    out_a = bg * conv_a * _silu(za)
    sa_ref[...] = xa_ref[:, pl.ds(CTX_A_ROWS + s - (CONV_A - 1), CONV_A - 1), :]

    pb = jnp.dot(h_in, wb_ref[...], preferred_element_type=F32)
    ab = pb[:, 0 * D_B:1 * D_B]
    gb = pb[:, 1 * D_B:2 * D_B]
    zb = pb[:, 2 * D_B:3 * D_B]
    xb_ref[:, pl.ds(CTX_B_ROWS - (CONV_B - 1), CONV_B - 1), :] = sb_in_ref[...]
    xb_ref[:, pl.ds(CTX_B_ROWS, s), :] = (ab * jax.nn.sigmoid(gb)).reshape(g, s, D_B)
    conv_b = bcb_ref[...]
    for kk in range(CONV_B):
        win = xb_ref[:, pl.ds(CTX_B_ROWS - (CONV_B - 1) + kk, s), :].reshape(t, D_B)
        conv_b = conv_b + wcb_ref[kk:kk + 1, :] * win
    out_b = _silu(_layernorm(conv_b, lng_ref[...], lnb_ref[...])) * _silu(zb)
    sb_ref[...] = xb_ref[:, pl.ds(CTX_B_ROWS + s - (CONV_B - 1), CONV_B - 1), :]

    px = jnp.dot(h_in, wx_ref[...], preferred_element_type=F32)
    q = px[:, :D_X] * ATTN_SCALE
    zx = px[:, D_X:]
    rows = N_XHEADS * s
    row_head = lax.broadcasted_iota(jnp.int32, (rows, D_X), 0) // s
    head_mask = row_head == _head_of_lane((rows, D_X), 1)
    for i in range(g):
        qi = q[i * s:(i + 1) * s, :]
        q_stack = jnp.where(head_mask, jnp.concatenate([qi] * N_XHEADS, axis=0), 0.0)
        kb = kc_ref[i].astype(BF16)
        vb = vc_ref[i].astype(BF16)
        sc = lax.dot_general(q_stack.astype(BF16), kb, (((1,), (1,)), ((), ())),
                             preferred_element_type=F32)
        e, l = _softmax_unnormalised(sc)
        o_all = jnp.dot(e.astype(BF16), vb, preferred_element_type=F32) / l
        o_all = jnp.where(head_mask, o_all, 0.0)
        oi = o_all[0:s, :]
        for h in range(1, N_XHEADS):
            oi = oi + o_all[h * s:(h + 1) * s, :]
        ox_ref[pl.ds(i * s, s), :] = oi
    out_x = ox_ref[...] * _silu(zx)

    y_ref[...] = _mix_tail(x, out_a, out_b, out_x, wout_ref, gfin_ref).reshape(g, s, D_MODEL)


def _sample_call(x, sa, sb, kc, vc, gnorm, wa, wb, wx, wca, bca, wcb, bcb, lng, lnb, wout,
                 gfin):
    nseq, s, _ = x.shape
    assert s == SUBLANES
    g = SAMPLE_GROUP
    const = lambda shape: pl.BlockSpec(shape, lambda i: (0,) * len(shape))
    grp = lambda shape: pl.BlockSpec((g,) + shape, lambda i: (i,) + (0,) * len(shape))
    in_specs = [
        grp((s, D_MODEL)),
        grp((CONV_A - 1, D_A)),
        grp((CONV_B - 1, D_B)),
        grp((N_MEM, D_X)),
        grp((N_MEM, D_X)),
        const((1, D_MODEL)),
        const((D_MODEL, 4 * D_A)),
        const((D_MODEL, 3 * D_B)),
        const((D_MODEL, 2 * D_X)),
        const((CONV_A, D_A)),
        const((1, D_A)),
        const((CONV_B, D_B)),
        const((1, D_B)),
        const((1, D_B)),
        const((1, D_B)),
        const((D_MODEL, D_MODEL)),
        const((1, D_MODEL)),
    ]
    out_shape = (
        jax.ShapeDtypeStruct((nseq, s, D_MODEL), F32),
        jax.ShapeDtypeStruct((nseq, CONV_A - 1, D_A), F32),
        jax.ShapeDtypeStruct((nseq, CONV_B - 1, D_B), F32),
    )
    out_specs = (
        grp((s, D_MODEL)),
        grp((CONV_A - 1, D_A)),
        grp((CONV_B - 1, D_B)),
    )
    scratch_shapes = [
        pltpu.VMEM((g, CTX_A_ROWS + s, D_A), F32),
        pltpu.VMEM((g, CTX_B_ROWS + s, D_B), F32),
        pltpu.VMEM((g * s, D_X), F32),
    ]
    return pl.pallas_call(
        _sample_kernel,
        out_shape=out_shape,
        grid=(nseq // g,),
        in_specs=in_specs,
        out_specs=out_specs,
        scratch_shapes=scratch_shapes,
        compiler_params=pltpu.CompilerParams(
            dimension_semantics=("arbitrary",),
            vmem_limit_bytes=VMEM_LIMIT_BYTES),
        name="sample_mixer",
    )(x, sa, sb, kc, vc, gnorm, wa, wb, wx, wca, bca, wcb, bcb, lng, lnb, wout, gfin)


def kernel(x_prompt, x_sample, state_conv_a, state_conv_b, cache_mem_k, cache_mem_v, mem_prompt, g_norm, w_in, w_conv_a, b_conv_a, w_conv_b, b_conv_b, ln_g, ln_b, w_out, g_mem, w_mem_k, w_mem_v, g_final):
    depth = g_norm.shape[0]
    assert depth == 1, "single-layer trunk only"
    bsz = x_prompt.shape[0]
    nseq = x_sample.shape[0]
    row = lambda v: v.reshape(1, -1)

    w_in_l = w_in[0]
    wa = w_in_l[:, :4 * D_A].astype(BF16)
    wb = w_in_l[:, 4 * D_A:4 * D_A + 3 * D_B].astype(BF16)
    wx = w_in_l[:, 4 * D_A + 3 * D_B:].astype(BF16)
    wout = w_out[0].astype(BF16)
    wmkv = jnp.concatenate([w_mem_k[0], w_mem_v[0]], axis=1).astype(BF16)
    shared = dict(gnorm=row(g_norm[0]), wa=wa, wb=wb, wx=wx, wca=w_conv_a[0],
                  bca=row(b_conv_a[0]), wcb=w_conv_b[0], bcb=row(b_conv_b[0]),
                  lng=row(ln_g[0]), lnb=row(ln_b[0]), wout=wout, gfin=row(g_final))

    y_p, pa, pb, pk, pv = _prompt_call(
        x_prompt, mem_prompt, gmem=row(g_mem[0]), wmkv=wmkv, **shared)
    y_s, sa, sb = _sample_call(
        x_sample, state_conv_a[0], state_conv_b[0],
        cache_mem_k[0].reshape(nseq, N_MEM, D_X), cache_mem_v[0].reshape(nseq, N_MEM, D_X),
        **shared)

    return (y_p, y_s, pa[None], pb[None],
            pk.reshape(1, bsz, N_MEM, N_XHEADS, XHEAD_DIM),
            pv.reshape(1, bsz, N_MEM, N_XHEADS, XHEAD_DIM),
            sa[None], sb[None])
```

```python
import functools

import jax
import jax.numpy as jnp
from jax import lax
from jax.experimental import pallas as pl
from jax.experimental.pallas import tpu as pltpu

D_MODEL = 1024
D_A = 384
D_B = 384
N_XHEADS = 4
XHEAD_DIM = 64
D_X = N_XHEADS * XHEAD_DIM
N_MEM = 256
CONV_A = 3
CONV_B = 31
EPS = 1e-6
ATTN_SCALE = XHEAD_DIM ** -0.5

SUBLANES = 8
LANES = 128
CONV_CHUNK_ROWS = 64

CTX_A_ROWS = SUBLANES
CTX_B_ROWS = 4 * SUBLANES

PROMPT_TILE = 256
SAMPLE_GROUP = 16
VMEM_LIMIT_BYTES = 56 * 1024 * 1024

F32 = jnp.float32
BF16 = jnp.bfloat16


def _rmsnorm(x, g):
    ms = jnp.mean(x * x, axis=-1, keepdims=True)
    return x * lax.rsqrt(ms + EPS) * g


def _layernorm(x, g, b):
    mu = jnp.mean(x, axis=-1, keepdims=True)
    xc = x - mu
    var = jnp.mean(xc * xc, axis=-1, keepdims=True)
    return xc * lax.rsqrt(var + EPS) * g + b


def _silu(x):
    return x * jax.nn.sigmoid(x)


def _causal_dwconv(xp_ref, w_ref, b_ref, y_ref, *, t, taps, lead):
    c = xp_ref.shape[1]
    for c0 in range(0, c, LANES):
        cols = slice(c0, c0 + LANES)
        for i0 in range(0, t, CONV_CHUNK_ROWS):
            acc = b_ref[:, cols]
            for r in range(SUBLANES):
                offs = [o for o in range(lead, lead + taps) if o % SUBLANES == r]
                if not offs:
                    continue
                rows = CONV_CHUNK_ROWS + (SUBLANES if r else 0)
                z = None
                for o in offs:
                    term = w_ref[o - lead:o - lead + 1, cols] * xp_ref[
                        pl.ds(i0 + o - r, rows), cols]
                    z = term if z is None else z + term
                acc = acc + (z[r:r + CONV_CHUNK_ROWS, :] if r else z)
            y_ref[pl.ds(i0, CONV_CHUNK_ROWS), cols] = acc


def _head_of_lane(shape, axis):
    return lax.broadcasted_iota(jnp.int32, shape, axis) // XHEAD_DIM


def _softmax_unnormalised(s):
    m = jnp.max(s, axis=-1, keepdims=True)
    e = jnp.exp(s - m)
    return e, jnp.sum(e, axis=-1, keepdims=True)


def _mix_tail(x, out_a, out_b, out_x, wout_ref, gfin_ref):
    mixed = jnp.concatenate(
        [out_a.astype(BF16), out_b.astype(BF16), out_x.astype(BF16)], axis=-1)
    y = x + jnp.dot(mixed, wout_ref[...], preferred_element_type=F32)
    return _rmsnorm(y, gfin_ref[...])


def _prompt_kernel(x_ref, mem_ref, gnorm_ref, wa_ref, wb_ref, wx_ref, wca_ref, bca_ref,
                   wcb_ref, bcb_ref, lng_ref, lnb_ref, wout_ref, gmem_ref, wmkv_ref,
                   gfin_ref,
                   y_ref, sa_ref, sb_ref, ko_ref, vo_ref,
                   xa_ref, xb_ref, ca_ref, cb_ref, kblk_ref, vblk_ref):
    j = pl.program_id(1)
    n_tiles = pl.num_programs(1)
    t = x_ref.shape[0]

    @pl.when(j == 0)
    def _start_of_sequence():
        xa_ref[pl.ds(0, CTX_A_ROWS), :] = jnp.zeros((CTX_A_ROWS, D_A), F32)
        xb_ref[pl.ds(0, CTX_B_ROWS), :] = jnp.zeros((CTX_B_ROWS, D_B), F32)
        m = _rmsnorm(mem_ref[...], gmem_ref[...]).astype(BF16)
        kv = jnp.dot(m, wmkv_ref[...], preferred_element_type=F32)
        k = kv[:, :D_X]
        v = kv[:, D_X:]
        ko_ref[...] = k
        vo_ref[...] = v
        kt = k.T
        row_head = _head_of_lane((D_X, N_MEM), 0)
        col_head = _head_of_lane((N_MEM, D_X), 1)
        for h in range(N_XHEADS):
            kblk_ref[:, h * N_MEM:(h + 1) * N_MEM] = jnp.where(
                row_head == h, kt, 0.0).astype(BF16)
            vblk_ref[h * N_MEM:(h + 1) * N_MEM, :] = jnp.where(
                col_head == h, v, 0.0).astype(BF16)

    x = x_ref[...]
    h_in = _rmsnorm(x, gnorm_ref[...]).astype(BF16)

    pa = jnp.dot(h_in, wa_ref[...], preferred_element_type=F32)
    bg = pa[:, 0 * D_A:1 * D_A]
    cg = pa[:, 1 * D_A:2 * D_A]
    va = pa[:, 2 * D_A:3 * D_A]
    za = pa[:, 3 * D_A:4 * D_A]
    xa_ref[pl.ds(CTX_A_ROWS, t), :] = cg * va
    _causal_dwconv(xa_ref, wca_ref, bca_ref, ca_ref, t=t, taps=CONV_A,
                   lead=CTX_A_ROWS - (CONV_A - 1))
    out_a = bg * ca_ref[...] * _silu(za)

    pb = jnp.dot(h_in, wb_ref[...], preferred_element_type=F32)
    ab = pb[:, 0 * D_B:1 * D_B]
    gb = pb[:, 1 * D_B:2 * D_B]
    zb = pb[:, 2 * D_B:3 * D_B]
    xb_ref[pl.ds(CTX_B_ROWS, t), :] = ab * jax.nn.sigmoid(gb)
    _causal_dwconv(xb_ref, wcb_ref, bcb_ref, cb_ref, t=t, taps=CONV_B,
                   lead=CTX_B_ROWS - (CONV_B - 1))
    out_b = _silu(_layernorm(cb_ref[...], lng_ref[...], lnb_ref[...])) * _silu(zb)

    px = jnp.dot(h_in, wx_ref[...], preferred_element_type=F32)
    q = (px[:, :D_X] * ATTN_SCALE).astype(BF16)
    zx = px[:, D_X:]
    s_all = jnp.dot(q, kblk_ref[...], preferred_element_type=F32)
    lane_head = _head_of_lane((t, D_X), 1)
    probs = []
    inv_l = jnp.zeros((t, D_X), F32)
    for h in range(N_XHEADS):
        e, l = _softmax_unnormalised(s_all[:, h * N_MEM:(h + 1) * N_MEM])
        probs.append(e.astype(BF16))
        inv_l = jnp.where(lane_head == h, 1.0 / l, inv_l)
    p_all = jnp.concatenate(probs, axis=-1)
    out_x = jnp.dot(p_all, vblk_ref[...], preferred_element_type=F32) * inv_l * _silu(zx)

    y_ref[...] = _mix_tail(x, out_a, out_b, out_x, wout_ref, gfin_ref)

    @pl.when(j == n_tiles - 1)
    def _emit_state():
        sa_ref[...] = xa_ref[pl.ds(t + CTX_A_ROWS - (CONV_A - 1), CONV_A - 1), :]
        sb_ref[...] = xb_ref[pl.ds(t + CTX_B_ROWS - (CONV_B - 1), CONV_B - 1), :]

    @pl.when(j < n_tiles - 1)
    def _carry_context():
        xa_ref[pl.ds(0, CTX_A_ROWS), :] = xa_ref[pl.ds(t, CTX_A_ROWS), :]
        xb_ref[pl.ds(0, CTX_B_ROWS), :] = xb_ref[pl.ds(t, CTX_B_ROWS), :]


def _prompt_call(x, mem, gnorm, wa, wb, wx, wca, bca, wcb, bcb, lng, lnb, wout, gmem, wmkv,
                 gfin):
    bsz, seq, _ = x.shape
    t = PROMPT_TILE
    n_tiles = seq // t
    const = lambda shape: pl.BlockSpec(shape, lambda b, j: (0,) * len(shape))
    in_specs = [
        pl.BlockSpec((None, t, D_MODEL), lambda b, j: (b, j, 0)),
        pl.BlockSpec((None, N_MEM, D_MODEL), lambda b, j: (b, 0, 0)),
        const((1, D_MODEL)),
        const((D_MODEL, 4 * D_A)),
        const((D_MODEL, 3 * D_B)),
        const((D_MODEL, 2 * D_X)),
        const((CONV_A, D_A)),
        const((1, D_A)),
        const((CONV_B, D_B)),
        const((1, D_B)),
        const((1, D_B)),
        const((1, D_B)),
        const((D_MODEL, D_MODEL)),
        const((1, D_MODEL)),
        const((D_MODEL, 2 * D_X)),
        const((1, D_MODEL)),
    ]
    out_shape = (
        jax.ShapeDtypeStruct((bsz, seq, D_MODEL), F32),
        jax.ShapeDtypeStruct((bsz, CONV_A - 1, D_A), F32),
        jax.ShapeDtypeStruct((bsz, CONV_B - 1, D_B), F32),
        jax.ShapeDtypeStruct((bsz, N_MEM, D_X), F32),
        jax.ShapeDtypeStruct((bsz, N_MEM, D_X), F32),
    )
    out_specs = (
        pl.BlockSpec((None, t, D_MODEL), lambda b, j: (b, j, 0)),
        pl.BlockSpec((None, CONV_A - 1, D_A), lambda b, j: (b, 0, 0)),
        pl.BlockSpec((None, CONV_B - 1, D_B), lambda b, j: (b, 0, 0)),
        pl.BlockSpec((None, N_MEM, D_X), lambda b, j: (b, 0, 0)),
        pl.BlockSpec((None, N_MEM, D_X), lambda b, j: (b, 0, 0)),
    )
    scratch_shapes = [
        pltpu.VMEM((t + CTX_A_ROWS, D_A), F32),
        pltpu.VMEM((t + CTX_B_ROWS, D_B), F32),
        pltpu.VMEM((t, D_A), F32),
        pltpu.VMEM((t, D_B), F32),
        pltpu.VMEM((D_X, N_XHEADS * N_MEM), BF16),
        pltpu.VMEM((N_XHEADS * N_MEM, D_X), BF16),
    ]
    return pl.pallas_call(
        _prompt_kernel,
        out_shape=out_shape,
        grid=(bsz, n_tiles),
        in_specs=in_specs,
        out_specs=out_specs,
        scratch_shapes=scratch_shapes,
        compiler_params=pltpu.CompilerParams(
            dimension_semantics=("arbitrary", "arbitrary"),
            vmem_limit_bytes=VMEM_LIMIT_BYTES),
        name="prompt_mixer",
    )(x, mem, gnorm, wa, wb, wx, wca, bca, wcb, bcb, lng, lnb, wout, gmem, wmkv, gfin)


def _sample_kernel(x_ref, sa_in_ref, sb_in_ref, kc_ref, vc_ref, gnorm_ref, wa_ref, wb_ref,
                   wx_ref, wca_ref, bca_ref, wcb_ref, bcb_ref, lng_ref, lnb_ref, wout_ref,
                   gfin_ref,
                   y_ref, sa_ref, sb_ref,
                   xa_ref, xb_ref, ox_ref):
    g, s, _ = x_ref.shape
    t = g * s
    x = x_ref[...].reshape(t, D_MODEL)
    h_in = _rmsnorm(x, gnorm_ref[...]).astype(BF16)

    pa = jnp.dot(h_in, wa_ref[...], preferred_element_type=F32)
    bg = pa[:, 0 * D_A:1 * D_A]
    cg = pa[:, 1 * D_A:2 * D_A]
    va = pa[:, 2 * D_A:3 * D_A]
    za = pa[:, 3 * D_A:4 * D_A]
    xa_ref[:, pl.ds(CTX_A_ROWS - (CONV_A - 1), CONV_A - 1), :] = sa_in_ref[...]
    xa_ref[:, pl.ds(CTX_A_ROWS, s), :] = (cg * va).reshape(g, s, D_A)
    conv_a = bca_ref[...]
    for kk in range(CONV_A):
        win = xa_ref[:, pl.ds(CTX_A_ROWS - (CONV_A - 1) + kk, s), :].reshape(t, D_A)
        conv_a = conv_a + wca_ref[kk:kk + 1, :] * win
    out_a = bg * conv_a * _silu(za)
    sa_ref[...] = xa_ref[:, pl.ds(CTX_A_ROWS + s - (CONV_A - 1), CONV_A - 1), :]

    pb = jnp.dot(h_in, wb_ref[...], preferred_element_type=F32)
    ab = pb[:, 0 * D_B:1 * D_B]
    gb = pb[:, 1 * D_B:2 * D_B]
    zb = pb[:, 2 * D_B:3 * D_B]
    xb_ref[:, pl.ds(CTX_B_ROWS - (CONV_B - 1), CONV_B - 1), :] = sb_in_ref[...]
    xb_ref[:, pl.ds(CTX_B_ROWS, s), :] = (ab * jax.nn.sigmoid(gb)).reshape(g, s, D_B)
    conv_b = bcb_ref[...]
    for kk in range(CONV_B):
        win = xb_ref[:, pl.ds(CTX_B_ROWS - (CONV_B - 1) + kk, s), :].reshape(t, D_B)
        conv_b = conv_b + wcb_ref[kk:kk + 1, :] * win
    out_b = _silu(_layernorm(conv_b, lng_ref[...], lnb_ref[...])) * _silu(zb)
    sb_ref[...] = xb_ref[:, pl.ds(CTX_B_ROWS + s - (CONV_B - 1), CONV_B - 1), :]

    px = jnp.dot(h_in, wx_ref[...], preferred_element_type=F32)
    q = px[:, :D_X] * ATTN_SCALE
    zx = px[:, D_X:]
    rows = N_XHEADS * s
    row_head = lax.broadcasted_iota(jnp.int32, (rows, D_X), 0) // s
    head_mask = row_head == _head_of_lane((rows, D_X), 1)
    for i in range(g):
        qi = q[i * s:(i + 1) * s, :]
        q_stack = jnp.where(head_mask, jnp.concatenate([qi] * N_XHEADS, axis=0), 0.0)
        kb = kc_ref[i].astype(BF16)
        vb = vc_ref[i].astype(BF16)
        sc = lax.dot_general(q_stack.astype(BF16), kb, (((1,), (1,)), ((), ())),
                             preferred_element_type=F32)
        e, l = _softmax_unnormalised(sc)
        o_all = jnp.dot(e.astype(BF16), vb, preferred_element_type=F32) / l
        o_all = jnp.where(head_mask, o_all, 0.0)
        oi = o_all[0:s, :]
        for h in range(1, N_XHEADS):
            oi = oi + o_all[h * s:(h + 1) * s, :]
        ox_ref[pl.ds(i * s, s), :] = oi
    out_x = ox_ref[...] * _silu(zx)

    y_ref[...] = _mix_tail(x, out_a, out_b, out_x, wout_ref, gfin_ref).reshape(g, s, D_MODEL)


def _sample_call(x, sa, sb, kc, vc, gnorm, wa, wb, wx, wca, bca, wcb, bcb, lng, lnb, wout,
                 gfin):
    nseq, s, _ = x.shape
    assert s == SUBLANES
    g = SAMPLE_GROUP
    const = lambda shape: pl.BlockSpec(shape, lambda i: (0,) * len(shape))
    grp = lambda shape: pl.BlockSpec((g,) + shape, lambda i: (i,) + (0,) * len(shape))
    in_specs = [
        grp((s, D_MODEL)),
        grp((CONV_A - 1, D_A)),
        grp((CONV_B - 1, D_B)),
        grp((N_MEM, D_X)),
        grp((N_MEM, D_X)),
        const((1, D_MODEL)),
        const((D_MODEL, 4 * D_A)),
        const((D_MODEL, 3 * D_B)),
        const((D_MODEL, 2 * D_X)),
        const((CONV_A, D_A)),
        const((1, D_A)),
        const((CONV_B, D_B)),
        const((1, D_B)),
        const((1, D_B)),
        const((1, D_B)),
        const((D_MODEL, D_MODEL)),
        const((1, D_MODEL)),
    ]
    out_shape = (
        jax.ShapeDtypeStruct((nseq, s, D_MODEL), F32),
        jax.ShapeDtypeStruct((nseq, CONV_A - 1, D_A), F32),
        jax.ShapeDtypeStruct((nseq, CONV_B - 1, D_B), F32),
    )
    out_specs = (
        grp((s, D_MODEL)),
        grp((CONV_A - 1, D_A)),
        grp((CONV_B - 1, D_B)),
    )
    scratch_shapes = [
        pltpu.VMEM((g, CTX_A_ROWS + s, D_A), F32),
        pltpu.VMEM((g, CTX_B_ROWS + s, D_B), F32),
        pltpu.VMEM((g * s, D_X), F32),
    ]
    return pl.pallas_call(
        _sample_kernel,
        out_shape=out_shape,
        grid=(nseq // g,),
        in_specs=in_specs,
        out_specs=out_specs,
        scratch_shapes=scratch_shapes,
        compiler_params=pltpu.CompilerParams(
            dimension_semantics=("arbitrary",),
            vmem_limit_bytes=VMEM_LIMIT_BYTES),
        name="sample_mixer",
    )(x, sa, sb, kc, vc, gnorm, wa, wb, wx, wca, bca, wcb, bcb, lng, lnb, wout, gfin)


def kernel(x_prompt, x_sample, state_conv_a, state_conv_b, cache_mem_k, cache_mem_v, mem_prompt, g_norm, w_in, w_conv_a, b_conv_a, w_conv_b, b_conv_b, ln_g, ln_b, w_out, g_mem, w_mem_k, w_mem_v, g_final):
    depth = g_norm.shape[0]
    assert depth == 1, "single-layer trunk only"
    bsz = x_prompt.shape[0]
    nseq = x_sample.shape[0]
    row = lambda v: v.reshape(1, -1)

    w_in_l = w_in[0]
    wa = w_in_l[:, :4 * D_A].astype(BF16)
    wb = w_in_l[:, 4 * D_A:4 * D_A + 3 * D_B].astype(BF16)
    wx = w_in_l[:, 4 * D_A + 3 * D_B:].astype(BF16)
    wout = w_out[0].astype(BF16)
    wmkv = jnp.concatenate([w_mem_k[0], w_mem_v[0]], axis=1).astype(BF16)
    shared = dict(gnorm=row(g_norm[0]), wa=wa, wb=wb, wx=wx, wca=w_conv_a[0],
                  bca=row(b_conv_a[0]), wcb=w_conv_b[0], bcb=row(b_conv_b[0]),
                  lng=row(ln_g[0]), lnb=row(ln_b[0]), wout=wout, gfin=row(g_final))

    y_p, pa, pb, pk, pv = _prompt_call(
        x_prompt, mem_prompt, gmem=row(g_mem[0]), wmkv=wmkv, **shared)
    y_s, sa, sb = _sample_call(
        x_sample, state_conv_a[0], state_conv_b[0],
        cache_mem_k[0].reshape(nseq, N_MEM, D_X), cache_mem_v[0].reshape(nseq, N_MEM, D_X),
        **shared)

    return (y_p, y_s, pa[None], pb[None],
            pk.reshape(1, bsz, N_MEM, N_XHEADS, XHEAD_DIM),
            pv.reshape(1, bsz, N_MEM, N_XHEADS, XHEAD_DIM),
            sa[None], sb[None])
```

```python
import functools

import jax
import jax.numpy as jnp
from jax import lax
from jax.experimental import pallas as pl
from jax.experimental.pallas import tpu as pltpu

D_MODEL = 1024
D_A = 384
D_B = 384
N_XHEADS = 4
XHEAD_DIM = 64
D_X = N_XHEADS * XHEAD_DIM
N_MEM = 256
CONV_A = 3
CONV_B = 31
EPS = 1e-6
ATTN_SCALE = XHEAD_DIM ** -0.5

SUBLANES = 8
LANES = 128
CONV_CHUNK_ROWS = 64
PROJ_CHUNK_COLS = 512

CTX_A_ROWS = SUBLANES
CTX_B_ROWS = 4 * SUBLANES

PROMPT_TILE = 256
SAMPLE_GROUP = 16
VMEM_LIMIT_BYTES = 56 * 1024 * 1024

F32 = jnp.float32
BF16 = jnp.bfloat16


def _rmsnorm(x, g):
    ms = jnp.mean(x * x, axis=-1, keepdims=True)
    return x * lax.rsqrt(ms + EPS) * g


def _layernorm(x, g, b):
    mu = jnp.mean(x, axis=-1, keepdims=True)
    xc = x - mu
    var = jnp.mean(xc * xc, axis=-1, keepdims=True)
    return xc * lax.rsqrt(var + EPS) * g + b


def _silu(x):
    return x * jax.nn.sigmoid(x)


def _zero_after(v):
    bits = pltpu.bitcast(v, jnp.uint32)
    sixteen = jnp.uint32(16)
    cleared = lax.shift_right_logical(lax.shift_right_logical(bits, sixteen), sixteen)
    return pltpu.bitcast(cleared, F32)


def _causal_dwconv(xp_ref, w_ref, b_ref, y_ref, *, t, taps, lead, lane_tiles=None,
                   after=None):
    c = xp_ref.shape[1]
    for c0 in range(0, c, LANES) if lane_tiles is None else lane_tiles:
        cols = slice(c0, c0 + LANES)
        for i0 in range(0, t, CONV_CHUNK_ROWS):
            acc = b_ref[:, cols] if after is None else b_ref[:, cols] + after
            for r in range(SUBLANES):
                offs = [o for o in range(lead, lead + taps) if o % SUBLANES == r]
                if not offs:
                    continue
                rows = CONV_CHUNK_ROWS + (SUBLANES if r else 0)
                z = None
                for o in offs:
                    term = w_ref[o - lead:o - lead + 1, cols] * xp_ref[
                        pl.ds(i0 + o - r, rows), cols]
                    z = term if z is None else z + term
                acc = acc + (z[r:r + CONV_CHUNK_ROWS, :] if r else z)
            y_ref[pl.ds(i0, CONV_CHUNK_ROWS), cols] = acc


def _head_of_lane(shape, axis):
    return lax.broadcasted_iota(jnp.int32, shape, axis) // XHEAD_DIM


def _softmax_unnormalised(s):
    m = jnp.max(s, axis=-1, keepdims=True)
    e = jnp.exp(s - m)
    return e, jnp.sum(e, axis=-1, keepdims=True)


def _mix_tail(x, out_a, out_b, out_x, wout_ref, gfin_ref):
    mixed = jnp.concatenate(
        [out_a.astype(BF16), out_b.astype(BF16), out_x.astype(BF16)], axis=-1)
    y = x + jnp.dot(mixed, wout_ref[...], preferred_element_type=F32)
    return _rmsnorm(y, gfin_ref[...])


def _projection_chunks(proj, weights):
    pieces = []
    for p_ref, w_ref in zip(proj, weights):
        n = w_ref.shape[1]
        for c0 in range(0, n, PROJ_CHUNK_COLS):
            pieces.append((p_ref, w_ref, c0, min(c0 + PROJ_CHUNK_COLS, n)))
    return pieces


def _prompt_step(x_ref, xprev_ref, y_ref, proj_write, proj_read, gnorm_ref, weights,
                 wca_ref, bca_ref, wcb_ref, bcb_ref, lng_ref, lnb_ref, wout_ref, gfin_ref,
                 xa_ref, xb_ref, ca_ref, cb_ref, kblk_ref, vblk_ref, h_ref, mixed_ref):
    pa_ref, pb_ref, px_ref = proj_read
    t = x_ref.shape[0]
    pieces = iter(_projection_chunks(proj_write, weights))

    def project_piece():
        p_ref, w_ref, c0, c1 = next(pieces)
        res = jnp.dot(h_ref[...], w_ref[:, c0:c1], preferred_element_type=F32)
        p_ref[:, c0:c1] = res
        return _zero_after(res[0:SUBLANES, 0:LANES])[0:1, :]

    def wide(zeros):
        return jnp.concatenate([zeros] * (D_B // LANES), axis=1)

    def conv_b(c0, started):
        _causal_dwconv(xb_ref, wcb_ref, bcb_ref, cb_ref, t=t, taps=CONV_B,
                       lead=CTX_B_ROWS - (CONV_B - 1), lane_tiles=(c0,), after=started)

    q = (px_ref[:, :D_X] * ATTN_SCALE).astype(BF16)
    s_all = jnp.dot(q, kblk_ref[...], preferred_element_type=F32)
    h_ref[...] = _rmsnorm(x_ref[...], gnorm_ref[...]).astype(BF16)

    started = project_piece()
    lane_head = _head_of_lane((t, D_X), 1)
    probs = []
    inv_l = jnp.zeros((t, D_X), F32)
    for h in range(N_XHEADS):
        e, l = _softmax_unnormalised(s_all[:, h * N_MEM:(h + 1) * N_MEM])
        probs.append(e.astype(BF16))
        inv_l = jnp.where(lane_head == h, 1.0 / l, inv_l)
    p_all = jnp.concatenate(probs, axis=-1)
    xb_ref[pl.ds(CTX_B_ROWS, t), :] = (
        pb_ref[:, 0:D_B] * jax.nn.sigmoid(pb_ref[:, D_B:2 * D_B]) + wide(started))
    conv_b(0 * LANES, started)

    started = project_piece()
    conv_b(1 * LANES, started)
    out_x = jnp.dot(p_all, vblk_ref[...], preferred_element_type=F32) * inv_l * _silu(
        px_ref[:, D_X:])
    mixed_ref[:, D_A + D_B:] = out_x.astype(BF16)

    started = project_piece()
    conv_b(2 * LANES, started)

    started = project_piece()
    xa_ref[pl.ds(CTX_A_ROWS, t), :] = (
        pa_ref[:, 1 * D_A:2 * D_A] * pa_ref[:, 2 * D_A:3 * D_A] + wide(started))
    _causal_dwconv(xa_ref, wca_ref, bca_ref, ca_ref, t=t, taps=CONV_A,
                   lead=CTX_A_ROWS - (CONV_A - 1))
    mixed_ref[:, 0:D_A] = (
        pa_ref[:, 0:D_A] * ca_ref[...] * _silu(pa_ref[:, 3 * D_A:4 * D_A])).astype(BF16)

    started = project_piece()
    mixed_ref[:, D_A:D_A + D_B] = (
        _silu(_layernorm(cb_ref[...] + wide(started), lng_ref[...], lnb_ref[...]))
        * _silu(pb_ref[:, 2 * D_B:3 * D_B])).astype(BF16)

    project_piece()
    xa_ref[pl.ds(0, CTX_A_ROWS), :] = xa_ref[pl.ds(t, CTX_A_ROWS), :]
    xb_ref[pl.ds(0, CTX_B_ROWS), :] = xb_ref[pl.ds(t, CTX_B_ROWS), :]
    y = xprev_ref[...] + jnp.dot(mixed_ref[...], wout_ref[...], preferred_element_type=F32)
    project_piece()
    y_ref[...] = _rmsnorm(y, gfin_ref[...])
    assert next(pieces, None) is None, "projection pieces left over"


def _prompt_kernel(x_ref, xprev_ref, mem_ref, gnorm_ref, wa_ref, wb_ref, wx_ref, wca_ref,
                   bca_ref, wcb_ref, bcb_ref, lng_ref, lnb_ref, wout_ref, gmem_ref, wmkv_ref,
                   gfin_ref,
                   y_ref, sa_ref, sb_ref, ko_ref, vo_ref,
                   xa_ref, xb_ref, ca_ref, cb_ref, kblk_ref, vblk_ref, h_ref, mixed_ref,
                   pa0_ref, pb0_ref, px0_ref, pa1_ref, pb1_ref, px1_ref, *, n_tiles):
    s = pl.program_id(0)
    j = lax.rem(jnp.maximum(s - 1, 0), n_tiles)
    proj0 = (pa0_ref, pb0_ref, px0_ref)
    proj1 = (pa1_ref, pb1_ref, px1_ref)

    @pl.when(s == 0)
    def _nothing_projected_yet():
        for ref in proj1:
            ref[...] = jnp.zeros(ref.shape, F32)

    @pl.when(j == 0)
    def _start_of_sequence():
        xa_ref[pl.ds(0, CTX_A_ROWS), :] = jnp.zeros((CTX_A_ROWS, D_A), F32)
        xb_ref[pl.ds(0, CTX_B_ROWS), :] = jnp.zeros((CTX_B_ROWS, D_B), F32)
        m = _rmsnorm(mem_ref[...], gmem_ref[...]).astype(BF16)
        kv = jnp.dot(m, wmkv_ref[...], preferred_element_type=F32)
        k = kv[:, :D_X]
        v = kv[:, D_X:]
        ko_ref[...] = k
        vo_ref[...] = v
        kt = k.T
        row_head = _head_of_lane((D_X, N_MEM), 0)
        col_head = _head_of_lane((N_MEM, D_X), 1)
        for h in range(N_XHEADS):
            kblk_ref[:, h * N_MEM:(h + 1) * N_MEM] = jnp.where(
                row_head == h, kt, 0.0).astype(BF16)
            vblk_ref[h * N_MEM:(h + 1) * N_MEM, :] = jnp.where(
                col_head == h, v, 0.0).astype(BF16)

    def step(proj_write, proj_read):
        _prompt_step(x_ref, xprev_ref, y_ref, proj_write, proj_read, gnorm_ref,
                     (wa_ref, wb_ref, wx_ref), wca_ref, bca_ref, wcb_ref, bcb_ref, lng_ref,
                     lnb_ref, wout_ref, gfin_ref, xa_ref, xb_ref, ca_ref, cb_ref, kblk_ref,
                     vblk_ref, h_ref, mixed_ref)

    parity = lax.rem(s, 2)
    pl.when(parity == 0)(functools.partial(step, proj0, proj1))
    pl.when(parity == 1)(functools.partial(step, proj1, proj0))

    @pl.when(jnp.logical_and(s > 0, j == n_tiles - 1))
    def _emit_state():
        sa_ref[...] = xa_ref[pl.ds(CTX_A_ROWS - (CONV_A - 1), CONV_A - 1), :]
        sb_ref[...] = xb_ref[pl.ds(CTX_B_ROWS - (CONV_B - 1), CONV_B - 1), :]


def _prompt_call(x, mem, gnorm, wa, wb, wx, wca, bca, wcb, bcb, lng, lnb, wout, gmem, wmkv,
                 gfin):
    bsz, seq, _ = x.shape
    t = PROMPT_TILE
    n_tiles = seq // t
    n_total = bsz * n_tiles

    def tile_index(tile):
        return (tile // n_tiles, lax.rem(tile, n_tiles), 0)

    projected = lambda s: tile_index(jnp.minimum(s, n_total - 1))
    mixed = lambda s: tile_index(jnp.maximum(s - 1, 0))
    mixed_row = lambda s: (jnp.maximum(s - 1, 0) // n_tiles, 0, 0)
    const = lambda shape: pl.BlockSpec(shape, lambda s: (0,) * len(shape))
    in_specs = [
        pl.BlockSpec((None, t, D_MODEL), projected),
        pl.BlockSpec((None, t, D_MODEL), mixed),
        pl.BlockSpec((None, N_MEM, D_MODEL), mixed_row),
        const((1, D_MODEL)),
        const((D_MODEL, 4 * D_A)),
        const((D_MODEL, 3 * D_B)),
        const((D_MODEL, 2 * D_X)),
        const((CONV_A, D_A)),
        const((1, D_A)),
        const((CONV_B, D_B)),
        const((1, D_B)),
        const((1, D_B)),
        const((1, D_B)),
        const((D_MODEL, D_MODEL)),
        const((1, D_MODEL)),
        const((D_MODEL, 2 * D_X)),
        const((1, D_MODEL)),
    ]
    out_shape = (
        jax.ShapeDtypeStruct((bsz, seq, D_MODEL), F32),
        jax.ShapeDtypeStruct((bsz, CONV_A - 1, D_A), F32),
        jax.ShapeDtypeStruct((bsz, CONV_B - 1, D_B), F32),
        jax.ShapeDtypeStruct((bsz, N_MEM, D_X), F32),
        jax.ShapeDtypeStruct((bsz, N_MEM, D_X), F32),
    )
    out_specs = (
        pl.BlockSpec((None, t, D_MODEL), mixed),
        pl.BlockSpec((None, CONV_A - 1, D_A), mixed_row),
        pl.BlockSpec((None, CONV_B - 1, D_B), mixed_row),
        pl.BlockSpec((None, N_MEM, D_X), mixed_row),
        pl.BlockSpec((None, N_MEM, D_X), mixed_row),
    )
    proj_shapes = [pltpu.VMEM((t, 4 * D_A), F32), pltpu.VMEM((t, 3 * D_B), F32),
                   pltpu.VMEM((t, 2 * D_X), F32)]
    scratch_shapes = [
        pltpu.VMEM((t + CTX_A_ROWS, D_A), F32),
        pltpu.VMEM((t + CTX_B_ROWS, D_B), F32),
        pltpu.VMEM((t, D_A), F32),
        pltpu.VMEM((t, D_B), F32),
        pltpu.VMEM((D_X, N_XHEADS * N_MEM), BF16),
        pltpu.VMEM((N_XHEADS * N_MEM, D_X), BF16),
        pltpu.VMEM((t, D_MODEL), BF16),
        pltpu.VMEM((t, D_MODEL), BF16),
    ] + proj_shapes + proj_shapes
    return pl.pallas_call(
        functools.partial(_prompt_kernel, n_tiles=n_tiles),
        out_shape=out_shape,
        grid=(n_total + 1,),
        in_specs=in_specs,
        out_specs=out_specs,
        scratch_shapes=scratch_shapes,
        compiler_params=pltpu.CompilerParams(
            dimension_semantics=("arbitrary",),
            vmem_limit_bytes=VMEM_LIMIT_BYTES),
        name="prompt_mixer",
    )(x, x, mem, gnorm, wa, wb, wx, wca, bca, wcb, bcb, lng, lnb, wout, gmem, wmkv, gfin)


def _sample_kernel(x_ref, sa_in_ref, sb_in_ref, kc_ref, vc_ref, gnorm_ref, wa_ref, wb_ref,
                   wx_ref, wca_ref, bca_ref, wcb_ref, bcb_ref, lng_ref, lnb_ref, wout_ref,
                   gfin_ref,
                   y_ref, sa_ref, sb_ref,
                   xa_ref, xb_ref, ox_ref):
    g, s, _ = x_ref.shape
    t = g * s
    x = x_ref[...].reshape(t, D_MODEL)
    h_in = _rmsnorm(x, gnorm_ref[...]).astype(BF16)

    pa = jnp.dot(h_in, wa_ref[...], preferred_element_type=F32)
    bg = pa[:, 0 * D_A:1 * D_A]
    cg = pa[:, 1 * D_A:2 * D_A]
    va = pa[:, 2 * D_A:3 * D_A]
    za = pa[:, 3 * D_A:4 * D_A]
    xa_ref[:, pl.ds(CTX_A_ROWS - (CONV_A - 1), CONV_A - 1), :] = sa_in_ref[...]
    xa_ref[:, pl.ds(CTX_A_ROWS, s), :] = (cg * va).reshape(g, s, D_A)
    conv_a = bca_ref[...]
    for kk in range(CONV_A):
        win = xa_ref[:, pl.ds(CTX_A_ROWS - (CONV_A - 1) + kk, s), :].reshape(t, D_A)
        conv_a = conv_a + wca_ref[kk:kk + 1, :] * win
    out_a = bg * conv_a * _silu(za)
    sa_ref[...] = xa_ref[:, pl.ds(CTX_A_ROWS + s - (CONV_A - 1), CONV_A - 1), :]

    pb = jnp.dot(h_in, wb_ref[...], preferred_element_type=F32)
    ab = pb[:, 0 * D_B:1 * D_B]
    gb = pb[:, 1 * D_B:2 * D_B]
    zb = pb[:, 2 * D_B:3 * D_B]
    xb_ref[:, pl.ds(CTX_B_ROWS - (CONV_B - 1), CONV_B - 1), :] = sb_in_ref[...]
    xb_ref[:, pl.ds(CTX_B_ROWS, s), :] = (ab * jax.nn.sigmoid(gb)).reshape(g, s, D_B)
    conv_b = bcb_ref[...]
    for kk in range(CONV_B):
        win = xb_ref[:, pl.ds(CTX_B_ROWS - (CONV_B - 1) + kk, s), :].reshape(t, D_B)
        conv_b = conv_b + wcb_ref[kk:kk + 1, :] * win
    out_b = _silu(_layernorm(conv_b, lng_ref[...], lnb_ref[...])) * _silu(zb)
    sb_ref[...] = xb_ref[:, pl.ds(CTX_B_ROWS + s - (CONV_B - 1), CONV_B - 1), :]

    px = jnp.dot(h_in, wx_ref[...], preferred_element_type=F32)
    q = px[:, :D_X] * ATTN_SCALE
    zx = px[:, D_X:]
    rows = N_XHEADS * s
    row_head = lax.broadcasted_iota(jnp.int32, (rows, D_X), 0) // s
    head_mask = row_head == _head_of_lane((rows, D_X), 1)
    for i in range(g):
        qi = q[i * s:(i + 1) * s, :]
        q_stack = jnp.where(head_mask, jnp.concatenate([qi] * N_XHEADS, axis=0), 0.0)
        kb = kc_ref[i].astype(BF16)
        vb = vc_ref[i].astype(BF16)
        sc = lax.dot_general(q_stack.astype(BF16), kb, (((1,), (1,)), ((), ())),
                             preferred_element_type=F32)
        e, l = _softmax_unnormalised(sc)
        o_all = jnp.dot(e.astype(BF16), vb, preferred_element_type=F32) / l
        o_all = jnp.where(head_mask, o_all, 0.0)
        oi = o_all[0:s, :]
        for h in range(1, N_XHEADS):
            oi = oi + o_all[h * s:(h + 1) * s, :]
        ox_ref[pl.ds(i * s, s), :] = oi
    out_x = ox_ref[...] * _silu(zx)

    y_ref[...] = _mix_tail(x, out_a, out_b, out_x, wout_ref, gfin_ref).reshape(g, s, D_MODEL)


def _sample_call(x, sa, sb, kc, vc, gnorm, wa, wb, wx, wca, bca, wcb, bcb, lng, lnb, wout,
                 gfin):
    nseq, s, _ = x.shape
    assert s == SUBLANES
    g = SAMPLE_GROUP
    const = lambda shape: pl.BlockSpec(shape, lambda i: (0,) * len(shape))
    grp = lambda shape: pl.BlockSpec((g,) + shape, lambda i: (i,) + (0,) * len(shape))
    in_specs = [
        grp((s, D_MODEL)),
        grp((CONV_A - 1, D_A)),
        grp((CONV_B - 1, D_B)),
        grp((N_MEM, D_X)),
        grp((N_MEM, D_X)),
        const((1, D_MODEL)),
        const((D_MODEL, 4 * D_A)),
        const((D_MODEL, 3 * D_B)),
        const((D_MODEL, 2 * D_X)),
        const((CONV_A, D_A)),
        const((1, D_A)),
        const((CONV_B, D_B)),
        const((1, D_B)),
        const((1, D_B)),
        const((1, D_B)),
        const((D_MODEL, D_MODEL)),
        const((1, D_MODEL)),
    ]
    out_shape = (
        jax.ShapeDtypeStruct((nseq, s, D_MODEL), F32),
        jax.ShapeDtypeStruct((nseq, CONV_A - 1, D_A), F32),
        jax.ShapeDtypeStruct((nseq, CONV_B - 1, D_B), F32),
    )
    out_specs = (
        grp((s, D_MODEL)),
        grp((CONV_A - 1, D_A)),
        grp((CONV_B - 1, D_B)),
    )
    scratch_shapes = [
        pltpu.VMEM((g, CTX_A_ROWS + s, D_A), F32),
        pltpu.VMEM((g, CTX_B_ROWS + s, D_B), F32),
        pltpu.VMEM((g * s, D_X), F32),
    ]
    return pl.pallas_call(
        _sample_kernel,
        out_shape=out_shape,
        grid=(nseq // g,),
        in_specs=in_specs,
        out_specs=out_specs,
        scratch_shapes=scratch_shapes,
        compiler_params=pltpu.CompilerParams(
            dimension_semantics=("arbitrary",),
            vmem_limit_bytes=VMEM_LIMIT_BYTES),
        name="sample_mixer",
    )(x, sa, sb, kc, vc, gnorm, wa, wb, wx, wca, bca, wcb, bcb, lng, lnb, wout, gfin)


def kernel(x_prompt, x_sample, state_conv_a, state_conv_b, cache_mem_k, cache_mem_v, mem_prompt, g_norm, w_in, w_conv_a, b_conv_a, w_conv_b, b_conv_b, ln_g, ln_b, w_out, g_mem, w_mem_k, w_mem_v, g_final):
    depth = g_norm.shape[0]
    assert depth == 1, "single-layer trunk only"
    bsz = x_prompt.shape[0]
    nseq = x_sample.shape[0]
    row = lambda v: v.reshape(1, -1)

    w_in_l = w_in[0]
    wa = w_in_l[:, :4 * D_A].astype(BF16)
    wb = w_in_l[:, 4 * D_A:4 * D_A + 3 * D_B].astype(BF16)
    wx = w_in_l[:, 4 * D_A + 3 * D_B:].astype(BF16)
    wout = w_out[0].astype(BF16)
    wmkv = jnp.concatenate([w_mem_k[0], w_mem_v[0]], axis=1).astype(BF16)
    shared = dict(gnorm=row(g_norm[0]), wa=wa, wb=wb, wx=wx, wca=w_conv_a[0],
                  bca=row(b_conv_a[0]), wcb=w_conv_b[0], bcb=row(b_conv_b[0]),
                  lng=row(ln_g[0]), lnb=row(ln_b[0]), wout=wout, gfin=row(g_final))

    y_p, pa, pb, pk, pv = _prompt_call(
        x_prompt, mem_prompt, gmem=row(g_mem[0]), wmkv=wmkv, **shared)
    y_s, sa, sb = _sample_call(
        x_sample, state_conv_a[0], state_conv_b[0],
        cache_mem_k[0].reshape(nseq, N_MEM, D_X), cache_mem_v[0].reshape(nseq, N_MEM, D_X),
        **shared)

    return (y_p, y_s, pa[None], pb[None],
            pk.reshape(1, bsz, N_MEM, N_XHEADS, XHEAD_DIM),
            pv.reshape(1, bsz, N_MEM, N_XHEADS, XHEAD_DIM),
            sa[None], sb[None])
```

```python
import jax
import jax.numpy as jnp
from jax import lax
from jax.experimental import pallas as pl
from jax.experimental.pallas import tpu as pltpu

D_MODEL = 1024
D_A = 384
D_B = 384
N_XHEADS = 4
XHEAD_DIM = 64
D_X = N_XHEADS * XHEAD_DIM
N_MEM = 256
CONV_A = 3
CONV_B = 31
EPS = 1e-6
ATTN_SCALE = XHEAD_DIM ** -0.5

COLS_A = (0, 4 * D_A)
COLS_B = (COLS_A[1], COLS_A[1] + 3 * D_B)
COLS_X = (COLS_B[1], COLS_B[1] + 2 * D_X)
D_IN = COLS_X[1]

SUBLANES = 8
LANES = 128
CONV_CHUNK_ROWS = 64

CTX_A_ROWS = SUBLANES
CTX_B_ROWS = 4 * SUBLANES

PROMPT_TILE = 256
SAMPLE_GROUP = 16
VMEM_LIMIT_BYTES = 56 * 1024 * 1024

F32 = jnp.float32
BF16 = jnp.bfloat16


def _rmsnorm(x, g):
    ms = jnp.mean(x * x, axis=-1, keepdims=True)
    return x * lax.rsqrt(ms + EPS) * g


def _layernorm(x, g, b):
    mu = jnp.mean(x, axis=-1, keepdims=True)
    xc = x - mu
    var = jnp.mean(xc * xc, axis=-1, keepdims=True)
    return xc * lax.rsqrt(var + EPS) * g + b


def _silu(x):
    return x * jax.nn.sigmoid(x)


def _causal_dwconv(xp_ref, w_ref, b_ref, y_ref, *, t, taps, lead):
    c = xp_ref.shape[1]
    for c0 in range(0, c, LANES):
        cols = slice(c0, c0 + LANES)
        for i0 in range(0, t, CONV_CHUNK_ROWS):
            acc = b_ref[:, cols]
            for r in range(SUBLANES):
                offs = [o for o in range(lead, lead + taps) if o % SUBLANES == r]
                if not offs:
                    continue
                rows = CONV_CHUNK_ROWS + (SUBLANES if r else 0)
                z = None
                for o in offs:
                    term = w_ref[o - lead:o - lead + 1, cols] * xp_ref[
                        pl.ds(i0 + o - r, rows), cols]
                    z = term if z is None else z + term
                acc = acc + (z[r:r + CONV_CHUNK_ROWS, :] if r else z)
            y_ref[pl.ds(i0, CONV_CHUNK_ROWS), cols] = acc


def _head_of_lane(shape, axis):
    return lax.broadcasted_iota(jnp.int32, shape, axis) // XHEAD_DIM


def _softmax_unnormalised(s):
    m = jnp.max(s, axis=-1, keepdims=True)
    e = jnp.exp(s - m)
    return e, jnp.sum(e, axis=-1, keepdims=True)


def _project(h_in, win_ref, cols):
    return jnp.dot(h_in, win_ref[:, cols[0]:cols[1]], preferred_element_type=F32)


def _mix_tail(x, out_a, out_b, out_x, wout_ref, gfin_ref):
    mixed = jnp.concatenate(
        [out_a.astype(BF16), out_b.astype(BF16), out_x.astype(BF16)], axis=-1)
    y = x + jnp.dot(mixed, wout_ref[...], preferred_element_type=F32)
    return _rmsnorm(y, gfin_ref[...])


def _prompt_kernel(x_ref, mem_ref, gnorm_ref, win_ref, wca_ref, bca_ref,
                   wcb_ref, bcb_ref, lng_ref, lnb_ref, wout_ref, gmem_ref, wmkv_ref,
                   gfin_ref,
                   y_ref, sa_ref, sb_ref, ko_ref, vo_ref,
                   xa_ref, xb_ref, ca_ref, cb_ref, kblk_ref, vblk_ref):
    j = pl.program_id(1)
    n_tiles = pl.num_programs(1)
    t = x_ref.shape[0]

    @pl.when(j == 0)
    def _start_of_sequence():
        xa_ref[pl.ds(0, CTX_A_ROWS), :] = jnp.zeros((CTX_A_ROWS, D_A), F32)
        xb_ref[pl.ds(0, CTX_B_ROWS), :] = jnp.zeros((CTX_B_ROWS, D_B), F32)
        m = _rmsnorm(mem_ref[...], gmem_ref[...]).astype(BF16)
        kv = jnp.dot(m, wmkv_ref[...], preferred_element_type=F32)
        k = kv[:, :D_X]
        v = kv[:, D_X:]
        ko_ref[...] = k
        vo_ref[...] = v
        kt = k.T
        row_head = _head_of_lane((D_X, N_MEM), 0)
        col_head = _head_of_lane((N_MEM, D_X), 1)
        for h in range(N_XHEADS):
            kblk_ref[:, h * N_MEM:(h + 1) * N_MEM] = jnp.where(
                row_head == h, kt, 0.0).astype(BF16)
            vblk_ref[h * N_MEM:(h + 1) * N_MEM, :] = jnp.where(
                col_head == h, v, 0.0).astype(BF16)

    x = x_ref[...]
    h_in = _rmsnorm(x, gnorm_ref[...]).astype(BF16)

    pa = _project(h_in, win_ref, COLS_A)
    bg = pa[:, 0 * D_A:1 * D_A]
    cg = pa[:, 1 * D_A:2 * D_A]
    va = pa[:, 2 * D_A:3 * D_A]
    za = pa[:, 3 * D_A:4 * D_A]
    xa_ref[pl.ds(CTX_A_ROWS, t), :] = cg * va
    _causal_dwconv(xa_ref, wca_ref, bca_ref, ca_ref, t=t, taps=CONV_A,
                   lead=CTX_A_ROWS - (CONV_A - 1))
    out_a = bg * ca_ref[...] * _silu(za)

    pb = _project(h_in, win_ref, COLS_B)
    ab = pb[:, 0 * D_B:1 * D_B]
    gb = pb[:, 1 * D_B:2 * D_B]
    zb = pb[:, 2 * D_B:3 * D_B]
    xb_ref[pl.ds(CTX_B_ROWS, t), :] = ab * jax.nn.sigmoid(gb)
    _causal_dwconv(xb_ref, wcb_ref, bcb_ref, cb_ref, t=t, taps=CONV_B,
                   lead=CTX_B_ROWS - (CONV_B - 1))
    out_b = _silu(_layernorm(cb_ref[...], lng_ref[...], lnb_ref[...])) * _silu(zb)

    px = _project(h_in, win_ref, COLS_X)
    q = (px[:, :D_X] * ATTN_SCALE).astype(BF16)
    zx = px[:, D_X:]
    s_all = jnp.dot(q, kblk_ref[...], preferred_element_type=F32)
    lane_head = _head_of_lane((t, D_X), 1)
    probs = []
    inv_l = jnp.zeros((t, D_X), F32)
    for h in range(N_XHEADS):
        e, l = _softmax_unnormalised(s_all[:, h * N_MEM:(h + 1) * N_MEM])
        probs.append(e.astype(BF16))
        inv_l = jnp.where(lane_head == h, 1.0 / l, inv_l)
    p_all = jnp.concatenate(probs, axis=-1)
    out_x = jnp.dot(p_all, vblk_ref[...], preferred_element_type=F32) * inv_l * _silu(zx)

    y_ref[...] = _mix_tail(x, out_a, out_b, out_x, wout_ref, gfin_ref)

    @pl.when(j == n_tiles - 1)
    def _emit_state():
        sa_ref[...] = xa_ref[pl.ds(t + CTX_A_ROWS - (CONV_A - 1), CONV_A - 1), :]
        sb_ref[...] = xb_ref[pl.ds(t + CTX_B_ROWS - (CONV_B - 1), CONV_B - 1), :]

    @pl.when(j < n_tiles - 1)
    def _carry_context():
        xa_ref[pl.ds(0, CTX_A_ROWS), :] = xa_ref[pl.ds(t, CTX_A_ROWS), :]
        xb_ref[pl.ds(0, CTX_B_ROWS), :] = xb_ref[pl.ds(t, CTX_B_ROWS), :]


def _prompt_call(x, mem, gnorm, win, wca, bca, wcb, bcb, lng, lnb, wout, gmem, wmkv, gfin):
    bsz, seq, _ = x.shape
    t = PROMPT_TILE
    n_tiles = seq // t
    const = lambda shape: pl.BlockSpec(shape, lambda b, j: (0,) * len(shape))
    in_specs = [
        pl.BlockSpec((None, t, D_MODEL), lambda b, j: (b, j, 0)),
        pl.BlockSpec((None, N_MEM, D_MODEL), lambda b, j: (b, 0, 0)),
        const((1, D_MODEL)),
        const((D_MODEL, D_IN)),
        const((CONV_A, D_A)),
        const((1, D_A)),
        const((CONV_B, D_B)),
        const((1, D_B)),
        const((1, D_B)),
        const((1, D_B)),
        const((D_MODEL, D_MODEL)),
        const((1, D_MODEL)),
        const((D_MODEL, 2 * D_X)),
        const((1, D_MODEL)),
    ]
    out_shape = (
        jax.ShapeDtypeStruct((bsz, seq, D_MODEL), F32),
        jax.ShapeDtypeStruct((bsz, CONV_A - 1, D_A), F32),
        jax.ShapeDtypeStruct((bsz, CONV_B - 1, D_B), F32),
        jax.ShapeDtypeStruct((bsz, N_MEM, D_X), F32),
        jax.ShapeDtypeStruct((bsz, N_MEM, D_X), F32),
    )
    out_specs = (
        pl.BlockSpec((None, t, D_MODEL), lambda b, j: (b, j, 0)),
        pl.BlockSpec((None, CONV_A - 1, D_A), lambda b, j: (b, 0, 0)),
        pl.BlockSpec((None, CONV_B - 1, D_B), lambda b, j: (b, 0, 0)),
        pl.BlockSpec((None, N_MEM, D_X), lambda b, j: (b, 0, 0)),
        pl.BlockSpec((None, N_MEM, D_X), lambda b, j: (b, 0, 0)),
    )
    scratch_shapes = [
        pltpu.VMEM((t + CTX_A_ROWS, D_A), F32),
        pltpu.VMEM((t + CTX_B_ROWS, D_B), F32),
        pltpu.VMEM((t, D_A), F32),
        pltpu.VMEM((t, D_B), F32),
        pltpu.VMEM((D_X, N_XHEADS * N_MEM), BF16),
        pltpu.VMEM((N_XHEADS * N_MEM, D_X), BF16),
    ]
    return pl.pallas_call(
        _prompt_kernel,
        out_shape=out_shape,
        grid=(bsz, n_tiles),
        in_specs=in_specs,
        out_specs=out_specs,
        scratch_shapes=scratch_shapes,
        compiler_params=pltpu.CompilerParams(
            dimension_semantics=("arbitrary", "arbitrary"),
            vmem_limit_bytes=VMEM_LIMIT_BYTES),
        name="prompt_mixer",
    )(x, mem, gnorm, win, wca, bca, wcb, bcb, lng, lnb, wout, gmem, wmkv, gfin)


def _sample_kernel(x_ref, sa_in_ref, sb_in_ref, kc_ref, vc_ref, gnorm_ref, win_ref,
                   wca_ref, bca_ref, wcb_ref, bcb_ref, lng_ref, lnb_ref, wout_ref,
                   gfin_ref,
                   y_ref, sa_ref, sb_ref,
                   xa_ref, xb_ref, ox_ref):
    g, s, _ = x_ref.shape
    t = g * s
    x = x_ref[...].reshape(t, D_MODEL)
    h_in = _rmsnorm(x, gnorm_ref[...]).astype(BF16)

    pa = _project(h_in, win_ref, COLS_A)
    bg = pa[:, 0 * D_A:1 * D_A]
    cg = pa[:, 1 * D_A:2 * D_A]
    va = pa[:, 2 * D_A:3 * D_A]
    za = pa[:, 3 * D_A:4 * D_A]
    xa_ref[:, pl.ds(CTX_A_ROWS - (CONV_A - 1), CONV_A - 1), :] = sa_in_ref[...]
    xa_ref[:, pl.ds(CTX_A_ROWS, s), :] = (cg * va).reshape(g, s, D_A)
    conv_a = bca_ref[...]
    for kk in range(CONV_A):
        win = xa_ref[:, pl.ds(CTX_A_ROWS - (CONV_A - 1) + kk, s), :].reshape(t, D_A)
        conv_a = conv_a + wca_ref[kk:kk + 1, :] * win
    out_a = bg * conv_a * _silu(za)
    sa_ref[...] = xa_ref[:, pl.ds(CTX_A_ROWS + s - (CONV_A - 1), CONV_A - 1), :]

    pb = _project(h_in, win_ref, COLS_B)
    ab = pb[:, 0 * D_B:1 * D_B]
    gb = pb[:, 1 * D_B:2 * D_B]
    zb = pb[:, 2 * D_B:3 * D_B]
    xb_ref[:, pl.ds(CTX_B_ROWS - (CONV_B - 1), CONV_B - 1), :] = sb_in_ref[...]
    xb_ref[:, pl.ds(CTX_B_ROWS, s), :] = (ab * jax.nn.sigmoid(gb)).reshape(g, s, D_B)
    conv_b = bcb_ref[...]
    for kk in range(CONV_B):
        win = xb_ref[:, pl.ds(CTX_B_ROWS - (CONV_B - 1) + kk, s), :].reshape(t, D_B)
        conv_b = conv_b + wcb_ref[kk:kk + 1, :] * win
    out_b = _silu(_layernorm(conv_b, lng_ref[...], lnb_ref[...])) * _silu(zb)
    sb_ref[...] = xb_ref[:, pl.ds(CTX_B_ROWS + s - (CONV_B - 1), CONV_B - 1), :]

    px = _project(h_in, win_ref, COLS_X)
    q = px[:, :D_X] * ATTN_SCALE
    zx = px[:, D_X:]
    rows = N_XHEADS * s
    cols = N_MEM * N_XHEADS
    row_head = lax.broadcasted_iota(jnp.int32, (rows, cols), 0) // s
    col_head = lax.rem(lax.broadcasted_iota(jnp.int32, (rows, cols), 1), N_XHEADS)
    same_head = row_head == col_head
    for i in range(g):
        qi = q[i * s:(i + 1) * s, :]
        q_heads = jnp.concatenate(
            [qi[:, h * XHEAD_DIM:(h + 1) * XHEAD_DIM] for h in range(N_XHEADS)], axis=0)
        kb = kc_ref[i].astype(BF16)
        vb = vc_ref[i].astype(BF16)
        sc = lax.dot_general(q_heads.astype(BF16), kb, (((1,), (1,)), ((), ())),
                             preferred_element_type=F32)
        e, l = _softmax_unnormalised(jnp.where(same_head, sc, -jnp.inf))
        o = jnp.dot(e.astype(BF16), vb, preferred_element_type=F32) / l
        ox_ref[pl.ds(i * s, s), :] = jnp.concatenate(
            [o[h * s:(h + 1) * s, :] for h in range(N_XHEADS)], axis=1)
    out_x = ox_ref[...] * _silu(zx)

    y_ref[...] = _mix_tail(x, out_a, out_b, out_x, wout_ref, gfin_ref).reshape(g, s, D_MODEL)


def _sample_call(x, sa, sb, kc, vc, gnorm, win, wca, bca, wcb, bcb, lng, lnb, wout, gfin):
    nseq, s, _ = x.shape
    assert s == SUBLANES
    g = SAMPLE_GROUP
    const = lambda shape: pl.BlockSpec(shape, lambda i: (0,) * len(shape),
                                       pipeline_mode=pl.Buffered(1))
    grp = lambda shape: pl.BlockSpec((g,) + shape, lambda i: (i,) + (0,) * len(shape))
    in_specs = [
        grp((s, D_MODEL)),
        grp((CONV_A - 1, D_A)),
        grp((CONV_B - 1, D_B)),
        grp((N_MEM * N_XHEADS, XHEAD_DIM)),
        grp((N_MEM * N_XHEADS, XHEAD_DIM)),
        const((1, D_MODEL)),
        const((D_MODEL, D_IN)),
        const((CONV_A, D_A)),
        const((1, D_A)),
        const((CONV_B, D_B)),
        const((1, D_B)),
        const((1, D_B)),
        const((1, D_B)),
        const((D_MODEL, D_MODEL)),
        const((1, D_MODEL)),
    ]
    out_shape = (
        jax.ShapeDtypeStruct((nseq, s, D_MODEL), F32),
        jax.ShapeDtypeStruct((nseq, CONV_A - 1, D_A), F32),
        jax.ShapeDtypeStruct((nseq, CONV_B - 1, D_B), F32),
    )
    out_specs = (
        grp((s, D_MODEL)),
        grp((CONV_A - 1, D_A)),
        grp((CONV_B - 1, D_B)),
    )
    scratch_shapes = [
        pltpu.VMEM((g, CTX_A_ROWS + s, D_A), F32),
        pltpu.VMEM((g, CTX_B_ROWS + s, D_B), F32),
        pltpu.VMEM((g * s, D_X), F32),
    ]
    return pl.pallas_call(
        _sample_kernel,
        out_shape=out_shape,
        grid=(nseq // g,),
        in_specs=in_specs,
        out_specs=out_specs,
        scratch_shapes=scratch_shapes,
        compiler_params=pltpu.CompilerParams(
            dimension_semantics=("arbitrary",),
            vmem_limit_bytes=VMEM_LIMIT_BYTES),
        name="sample_mixer",
    )(x, sa, sb, kc, vc, gnorm, win, wca, bca, wcb, bcb, lng, lnb, wout, gfin)


def kernel(x_prompt, x_sample, state_conv_a, state_conv_b, cache_mem_k, cache_mem_v, mem_prompt, g_norm, w_in, w_conv_a, b_conv_a, w_conv_b, b_conv_b, ln_g, ln_b, w_out, g_mem, w_mem_k, w_mem_v, g_final):
    depth = g_norm.shape[0]
    assert depth == 1, "single-layer trunk only"
    bsz = x_prompt.shape[0]
    nseq = x_sample.shape[0]
    row = lambda v: v.reshape(1, -1)

    wout = w_out[0].astype(BF16)
    wmkv = jnp.concatenate([w_mem_k[0], w_mem_v[0]], axis=1).astype(BF16)
    shared = dict(gnorm=row(g_norm[0]), win=w_in[0].astype(BF16), wca=w_conv_a[0],
                  bca=row(b_conv_a[0]), wcb=w_conv_b[0], bcb=row(b_conv_b[0]),
                  lng=row(ln_g[0]), lnb=row(ln_b[0]), wout=wout, gfin=row(g_final))

    y_p, pa, pb, pk, pv = _prompt_call(
        x_prompt, mem_prompt, gmem=row(g_mem[0]), wmkv=wmkv, **shared)
    y_s, sa, sb = _sample_call(
        x_sample, state_conv_a[0], state_conv_b[0],
        cache_mem_k[0].reshape(nseq, N_MEM * N_XHEADS, XHEAD_DIM),
        cache_mem_v[0].reshape(nseq, N_MEM * N_XHEADS, XHEAD_DIM),
        **shared)

    return (y_p, y_s, pa[None], pb[None],
            pk.reshape(1, bsz, N_MEM, N_XHEADS, XHEAD_DIM),
            pv.reshape(1, bsz, N_MEM, N_XHEADS, XHEAD_DIM),
            sa[None], sb[None])
```

```python
import jax
import jax.numpy as jnp
from jax import lax
from jax.experimental import pallas as pl
from jax.experimental.pallas import tpu as pltpu

D_MODEL = 1024
D_A = 384
D_B = 384
N_XHEADS = 4
XHEAD_DIM = 64
D_X = N_XHEADS * XHEAD_DIM
N_MEM = 256
CONV_A = 3
CONV_B = 31
EPS = 1e-6
ATTN_SCALE = XHEAD_DIM ** -0.5

COLS_A = (0, 4 * D_A)
COLS_B = (COLS_A[1], COLS_A[1] + 3 * D_B)
COLS_X = (COLS_B[1], COLS_B[1] + 2 * D_X)
D_IN = COLS_X[1]

SUBLANES = 8
LANES = 128
CONV_CHUNK_ROWS = 64

CTX_A_ROWS = SUBLANES
CTX_B_ROWS = 4 * SUBLANES

PROMPT_TILE = 256
SAMPLE_GROUP = 16
VMEM_LIMIT_BYTES = 56 * 1024 * 1024

F32 = jnp.float32
BF16 = jnp.bfloat16


def _rmsnorm(x, g):
    ms = jnp.mean(x * x, axis=-1, keepdims=True)
    return x * lax.rsqrt(ms + EPS) * g


def _layernorm(x, g, b):
    mu = jnp.mean(x, axis=-1, keepdims=True)
    xc = x - mu
    var = jnp.mean(xc * xc, axis=-1, keepdims=True)
    return xc * lax.rsqrt(var + EPS) * g + b


def _silu(x):
    return x * jax.nn.sigmoid(x)


def _causal_dwconv(xp_ref, w_ref, b_ref, y_ref, *, t, taps, lead):
    c = xp_ref.shape[1]
    for c0 in range(0, c, LANES):
        cols = slice(c0, c0 + LANES)
        for i0 in range(0, t, CONV_CHUNK_ROWS):
            acc = b_ref[:, cols]
            for r in range(SUBLANES):
                offs = [o for o in range(lead, lead + taps) if o % SUBLANES == r]
                if not offs:
                    continue
                rows = CONV_CHUNK_ROWS + (SUBLANES if r else 0)
                z = None
                for o in offs:
                    term = w_ref[o - lead:o - lead + 1, cols] * xp_ref[
                        pl.ds(i0 + o - r, rows), cols]
                    z = term if z is None else z + term
                acc = acc + (z[r:r + CONV_CHUNK_ROWS, :] if r else z)
            y_ref[pl.ds(i0, CONV_CHUNK_ROWS), cols] = acc


def _head_of_lane(shape, axis):
    return lax.broadcasted_iota(jnp.int32, shape, axis) // XHEAD_DIM


def _softmax_unnormalised(s):
    m = jnp.max(s, axis=-1, keepdims=True)
    e = jnp.exp(s - m)
    return e, jnp.sum(e, axis=-1, keepdims=True)


def _project(h_in, win_ref, cols):
    return jnp.dot(h_in, win_ref[:, cols[0]:cols[1]], preferred_element_type=F32)


def _mix_tail(x, out_a, out_b, out_x, wout_ref, gfin_ref):
    mixed = jnp.concatenate(
        [out_a.astype(BF16), out_b.astype(BF16), out_x.astype(BF16)], axis=-1)
    y = x + jnp.dot(mixed, wout_ref[...], preferred_element_type=F32)
    return _rmsnorm(y, gfin_ref[...])


def _prompt_kernel(x_ref, mem_ref, gnorm_ref, win_ref, wca_ref, bca_ref,
                   wcb_ref, bcb_ref, lng_ref, lnb_ref, wout_ref, gmem_ref, wmkv_ref,
                   gfin_ref,
                   y_ref, sa_ref, sb_ref, ko_ref, vo_ref,
                   xa_ref, xb_ref, ca_ref, cb_ref, kblk_ref, vblk_ref):
    j = pl.program_id(1)
    n_tiles = pl.num_programs(1)
    t = x_ref.shape[0]

    @pl.when(j == 0)
    def _start_of_sequence():
        xa_ref[pl.ds(0, CTX_A_ROWS), :] = jnp.zeros((CTX_A_ROWS, D_A), F32)
        xb_ref[pl.ds(0, CTX_B_ROWS), :] = jnp.zeros((CTX_B_ROWS, D_B), F32)
        m = _rmsnorm(mem_ref[...], gmem_ref[...]).astype(BF16)
        kv = jnp.dot(m, wmkv_ref[...], preferred_element_type=F32)
        v = kv[:, D_X:]
        kt = kv[:, :D_X].T
        ko_ref[...] = kt
        vo_ref[...] = v.T
        row_head = _head_of_lane((D_X, N_MEM), 0)
        col_head = _head_of_lane((N_MEM, D_X), 1)
        for h in range(N_XHEADS):
            kblk_ref[:, h * N_MEM:(h + 1) * N_MEM] = jnp.where(
                row_head == h, kt, 0.0).astype(BF16)
            vblk_ref[h * N_MEM:(h + 1) * N_MEM, :] = jnp.where(
                col_head == h, v, 0.0).astype(BF16)

    x = x_ref[...]
    h_in = _rmsnorm(x, gnorm_ref[...]).astype(BF16)

    pa = _project(h_in, win_ref, COLS_A)
    bg = pa[:, 0 * D_A:1 * D_A]
    cg = pa[:, 1 * D_A:2 * D_A]
    va = pa[:, 2 * D_A:3 * D_A]
    za = pa[:, 3 * D_A:4 * D_A]
    xa_ref[pl.ds(CTX_A_ROWS, t), :] = cg * va
    _causal_dwconv(xa_ref, wca_ref, bca_ref, ca_ref, t=t, taps=CONV_A,
                   lead=CTX_A_ROWS - (CONV_A - 1))
    out_a = bg * ca_ref[...] * _silu(za)

    pb = _project(h_in, win_ref, COLS_B)
    ab = pb[:, 0 * D_B:1 * D_B]
    gb = pb[:, 1 * D_B:2 * D_B]
    zb = pb[:, 2 * D_B:3 * D_B]
    xb_ref[pl.ds(CTX_B_ROWS, t), :] = ab * jax.nn.sigmoid(gb)
    _causal_dwconv(xb_ref, wcb_ref, bcb_ref, cb_ref, t=t, taps=CONV_B,
                   lead=CTX_B_ROWS - (CONV_B - 1))
    out_b = _silu(_layernorm(cb_ref[...], lng_ref[...], lnb_ref[...])) * _silu(zb)

    px = _project(h_in, win_ref, COLS_X)
    q = (px[:, :D_X] * ATTN_SCALE).astype(BF16)
    zx = px[:, D_X:]
    s_all = jnp.dot(q, kblk_ref[...], preferred_element_type=F32)
    lane_head = _head_of_lane((t, D_X), 1)
    probs = []
    inv_l = jnp.zeros((t, D_X), F32)
    for h in range(N_XHEADS):
        e, l = _softmax_unnormalised(s_all[:, h * N_MEM:(h + 1) * N_MEM])
        probs.append(e.astype(BF16))
        inv_l = jnp.where(lane_head == h, 1.0 / l, inv_l)
    p_all = jnp.concatenate(probs, axis=-1)
    out_x = jnp.dot(p_all, vblk_ref[...], preferred_element_type=F32) * inv_l * _silu(zx)

    y_ref[...] = _mix_tail(x, out_a, out_b, out_x, wout_ref, gfin_ref)

    @pl.when(j == n_tiles - 1)
    def _emit_state():
        sa_ref[...] = xa_ref[pl.ds(t + CTX_A_ROWS - (CONV_A - 1), CONV_A - 1), :]
        sb_ref[...] = xb_ref[pl.ds(t + CTX_B_ROWS - (CONV_B - 1), CONV_B - 1), :]

    @pl.when(j < n_tiles - 1)
    def _carry_context():
        xa_ref[pl.ds(0, CTX_A_ROWS), :] = xa_ref[pl.ds(t, CTX_A_ROWS), :]
        xb_ref[pl.ds(0, CTX_B_ROWS), :] = xb_ref[pl.ds(t, CTX_B_ROWS), :]


def _prompt_call(x, mem, gnorm, win, wca, bca, wcb, bcb, lng, lnb, wout, gmem, wmkv, gfin):
    bsz, seq, _ = x.shape
    t = PROMPT_TILE
    n_tiles = seq // t
    const = lambda shape: pl.BlockSpec(shape, lambda b, j: (0,) * len(shape))
    in_specs = [
        pl.BlockSpec((None, t, D_MODEL), lambda b, j: (b, j, 0)),
        pl.BlockSpec((None, N_MEM, D_MODEL), lambda b, j: (b, 0, 0)),
        const((1, D_MODEL)),
        const((D_MODEL, D_IN)),
        const((CONV_A, D_A)),
        const((1, D_A)),
        const((CONV_B, D_B)),
        const((1, D_B)),
        const((1, D_B)),
        const((1, D_B)),
        const((D_MODEL, D_MODEL)),
        const((1, D_MODEL)),
        const((D_MODEL, 2 * D_X)),
        const((1, D_MODEL)),
    ]
    out_shape = (
        jax.ShapeDtypeStruct((bsz, seq, D_MODEL), F32),
        jax.ShapeDtypeStruct((bsz, CONV_A - 1, D_A), F32),
        jax.ShapeDtypeStruct((bsz, CONV_B - 1, D_B), F32),
        jax.ShapeDtypeStruct((bsz, D_X, N_MEM), F32),
        jax.ShapeDtypeStruct((bsz, D_X, N_MEM), F32),
    )
    out_specs = (
        pl.BlockSpec((None, t, D_MODEL), lambda b, j: (b, j, 0)),
        pl.BlockSpec((None, CONV_A - 1, D_A), lambda b, j: (b, 0, 0)),
        pl.BlockSpec((None, CONV_B - 1, D_B), lambda b, j: (b, 0, 0)),
        pl.BlockSpec((None, D_X, N_MEM), lambda b, j: (b, 0, 0)),
        pl.BlockSpec((None, D_X, N_MEM), lambda b, j: (b, 0, 0)),
    )
    scratch_shapes = [
        pltpu.VMEM((t + CTX_A_ROWS, D_A), F32),
        pltpu.VMEM((t + CTX_B_ROWS, D_B), F32),
        pltpu.VMEM((t, D_A), F32),
        pltpu.VMEM((t, D_B), F32),
        pltpu.VMEM((D_X, N_XHEADS * N_MEM), BF16),
        pltpu.VMEM((N_XHEADS * N_MEM, D_X), BF16),
    ]
    return pl.pallas_call(
        _prompt_kernel,
        out_shape=out_shape,
        grid=(bsz, n_tiles),
        in_specs=in_specs,
        out_specs=out_specs,
        scratch_shapes=scratch_shapes,
        compiler_params=pltpu.CompilerParams(
            dimension_semantics=("arbitrary", "arbitrary"),
            vmem_limit_bytes=VMEM_LIMIT_BYTES),
        name="prompt_mixer",
    )(x, mem, gnorm, win, wca, bca, wcb, bcb, lng, lnb, wout, gmem, wmkv, gfin)


def _sample_kernel(x_ref, sa_in_ref, sb_in_ref, kc_ref, vc_ref, gnorm_ref, win_ref,
                   wca_ref, bca_ref, wcb_ref, bcb_ref, lng_ref, lnb_ref, wout_ref,
                   gfin_ref,
                   y_ref, sa_ref, sb_ref,
                   xa_ref, xb_ref, ox_ref):
    g, s, _ = x_ref.shape
    t = g * s
    x = x_ref[...].reshape(t, D_MODEL)
    h_in = _rmsnorm(x, gnorm_ref[...]).astype(BF16)

    pa = _project(h_in, win_ref, COLS_A)
    bg = pa[:, 0 * D_A:1 * D_A]
    cg = pa[:, 1 * D_A:2 * D_A]
    va = pa[:, 2 * D_A:3 * D_A]
    za = pa[:, 3 * D_A:4 * D_A]
    xa_ref[:, pl.ds(CTX_A_ROWS - (CONV_A - 1), CONV_A - 1), :] = sa_in_ref[...]
    xa_ref[:, pl.ds(CTX_A_ROWS, s), :] = (cg * va).reshape(g, s, D_A)
    conv_a = bca_ref[...]
    for kk in range(CONV_A):
        win = xa_ref[:, pl.ds(CTX_A_ROWS - (CONV_A - 1) + kk, s), :].reshape(t, D_A)
        conv_a = conv_a + wca_ref[kk:kk + 1, :] * win
    out_a = bg * conv_a * _silu(za)
    sa_ref[...] = xa_ref[:, pl.ds(CTX_A_ROWS + s - (CONV_A - 1), CONV_A - 1), :]

    pb = _project(h_in, win_ref, COLS_B)
    ab = pb[:, 0 * D_B:1 * D_B]
    gb = pb[:, 1 * D_B:2 * D_B]
    zb = pb[:, 2 * D_B:3 * D_B]
    xb_ref[:, pl.ds(CTX_B_ROWS - (CONV_B - 1), CONV_B - 1), :] = sb_in_ref[...]
    xb_ref[:, pl.ds(CTX_B_ROWS, s), :] = (ab * jax.nn.sigmoid(gb)).reshape(g, s, D_B)
    conv_b = bcb_ref[...]
    for kk in range(CONV_B):
        win = xb_ref[:, pl.ds(CTX_B_ROWS - (CONV_B - 1) + kk, s), :].reshape(t, D_B)
        conv_b = conv_b + wcb_ref[kk:kk + 1, :] * win
    out_b = _silu(_layernorm(conv_b, lng_ref[...], lnb_ref[...])) * _silu(zb)
    sb_ref[...] = xb_ref[:, pl.ds(CTX_B_ROWS + s - (CONV_B - 1), CONV_B - 1), :]

    px = _project(h_in, win_ref, COLS_X)
    q = px[:, :D_X] * ATTN_SCALE
    zx = px[:, D_X:]
    rows = N_XHEADS * s
    row_head = lax.broadcasted_iota(jnp.int32, (rows, D_X), 0) // s
    head_mask = row_head == _head_of_lane((rows, D_X), 1)
    for i in range(g):
        qi = q[i * s:(i + 1) * s, :]
        q_stack = jnp.where(head_mask, jnp.concatenate([qi] * N_XHEADS, axis=0), 0.0)
        kt = kc_ref[i].astype(BF16)
        vt = vc_ref[i].astype(BF16)
        sc = jnp.dot(q_stack.astype(BF16), kt, preferred_element_type=F32)
        e, l = _softmax_unnormalised(sc)
        o_all = lax.dot_general(e.astype(BF16), vt, (((1,), (1,)), ((), ())),
                                preferred_element_type=F32) / l
        o_all = jnp.where(head_mask, o_all, 0.0)
        oi = o_all[0:s, :]
        for h in range(1, N_XHEADS):
            oi = oi + o_all[h * s:(h + 1) * s, :]
        ox_ref[pl.ds(i * s, s), :] = oi
    out_x = ox_ref[...] * _silu(zx)

    y_ref[...] = _mix_tail(x, out_a, out_b, out_x, wout_ref, gfin_ref).reshape(g, s, D_MODEL)


def _sample_call(x, sa, sb, kc, vc, gnorm, win, wca, bca, wcb, bcb, lng, lnb, wout, gfin):
    nseq, s, _ = x.shape
    assert s == SUBLANES
    g = SAMPLE_GROUP
    const = lambda shape: pl.BlockSpec(shape, lambda i: (0,) * len(shape),
                                       pipeline_mode=pl.Buffered(1))
    grp = lambda shape: pl.BlockSpec((g,) + shape, lambda i: (i,) + (0,) * len(shape))
    in_specs = [
        grp((s, D_MODEL)),
        grp((CONV_A - 1, D_A)),
        grp((CONV_B - 1, D_B)),
        grp((D_X, N_MEM)),
        grp((D_X, N_MEM)),
        const((1, D_MODEL)),
        const((D_MODEL, D_IN)),
        const((CONV_A, D_A)),
        const((1, D_A)),
        const((CONV_B, D_B)),
        const((1, D_B)),
        const((1, D_B)),
        const((1, D_B)),
        const((D_MODEL, D_MODEL)),
        const((1, D_MODEL)),
    ]
    out_shape = (
        jax.ShapeDtypeStruct((nseq, s, D_MODEL), F32),
        jax.ShapeDtypeStruct((nseq, CONV_A - 1, D_A), F32),
        jax.ShapeDtypeStruct((nseq, CONV_B - 1, D_B), F32),
    )
    out_specs = (
        grp((s, D_MODEL)),
        grp((CONV_A - 1, D_A)),
        grp((CONV_B - 1, D_B)),
    )
    scratch_shapes = [
        pltpu.VMEM((g, CTX_A_ROWS + s, D_A), F32),
        pltpu.VMEM((g, CTX_B_ROWS + s, D_B), F32),
        pltpu.VMEM((g * s, D_X), F32),
    ]
    return pl.pallas_call(
        _sample_kernel,
        out_shape=out_shape,
        grid=(nseq // g,),
        in_specs=in_specs,
        out_specs=out_specs,
        scratch_shapes=scratch_shapes,
        compiler_params=pltpu.CompilerParams(
            dimension_semantics=("arbitrary",),
            vmem_limit_bytes=VMEM_LIMIT_BYTES),
        name="sample_mixer",
    )(x, sa, sb, kc, vc, gnorm, win, wca, bca, wcb, bcb, lng, lnb, wout, gfin)


def kernel(x_prompt, x_sample, state_conv_a, state_conv_b, cache_mem_k, cache_mem_v, mem_prompt, g_norm, w_in, w_conv_a, b_conv_a, w_conv_b, b_conv_b, ln_g, ln_b, w_out, g_mem, w_mem_k, w_mem_v, g_final):
    depth = g_norm.shape[0]
    assert depth == 1, "single-layer trunk only"
    bsz = x_prompt.shape[0]
    nseq = x_sample.shape[0]
    row = lambda v: v.reshape(1, -1)

    wout = w_out[0].astype(BF16)
    wmkv = jnp.concatenate([w_mem_k[0], w_mem_v[0]], axis=1).astype(BF16)
    shared = dict(gnorm=row(g_norm[0]), win=w_in[0].astype(BF16), wca=w_conv_a[0],
                  bca=row(b_conv_a[0]), wcb=w_conv_b[0], bcb=row(b_conv_b[0]),
                  lng=row(ln_g[0]), lnb=row(ln_b[0]), wout=wout, gfin=row(g_final))

    def transposed(cache):
        return jnp.transpose(cache, (0, 2, 3, 1)).reshape(cache.shape[0], D_X, N_MEM)

    def untransposed(kt):
        n = kt.shape[0]
        return jnp.transpose(kt.reshape(n, N_XHEADS, XHEAD_DIM, N_MEM), (0, 3, 1, 2))[None]

    y_p, pa, pb, pkt, pvt = _prompt_call(
        x_prompt, mem_prompt, gmem=row(g_mem[0]), wmkv=wmkv, **shared)
    y_s, sa, sb = _sample_call(
        x_sample, state_conv_a[0], state_conv_b[0],
        transposed(cache_mem_k[0]), transposed(cache_mem_v[0]), **shared)

    return (y_p, y_s, pa[None], pb[None], untransposed(pkt), untransposed(pvt),
            sa[None], sb[None])
```
